```python
import math
import jax
import jax.numpy as jnp
from jax import lax
import numpy as np

D_MODEL = 2048
BATCH = 4
SEQ = 2048
DEPTH = 4

N_HEADS = 16
HEAD_DIM = D_MODEL // N_HEADS
N_MIXERS = 4
D_FF = 4 * D_MODEL
CONV_WIDTH = 3
RMS_EPS = 1e-6
N_BUCKETS = 32
MAX_DISTANCE = 128
DILATED_GROUPS = ((128, 1), (512, 4), (2048, 16))
MOBA_BLOCK = 256
MOBA_TOPK = 3
MOBA_Q_CHUNK = 16
DIFF_HEAD_DIM = HEAD_DIM // 2
Q_BLOCK = 128
NEG = -1e30

kernel_name = 'hybrid_interleaved_dilated_moba_diff_fox'


def layers_of(m):
    return len(range(m, DEPTH, N_MIXERS))


def rms_norm(x, g):
    xf = x.astype(jnp.float32)
    y = xf * lax.rsqrt(jnp.mean(xf * xf, axis=-1, keepdims=True) + RMS_EPS)
    return (y * g.astype(jnp.float32)).astype(x.dtype)


def rel_bucket(dist):
    dist = jnp.maximum(dist, 0)
    max_exact = N_BUCKETS // 2
    d = jnp.maximum(dist, 1).astype(jnp.float32)
    large = max_exact + (jnp.log(d / max_exact) / math.log(MAX_DISTANCE / max_exact)
                         * (N_BUCKETS - max_exact)).astype(jnp.int32)
    large = jnp.minimum(large, N_BUCKETS - 1)
    return jnp.where(dist < max_exact, dist, large)


def rel_bias_heads(rel_bias, dist):
    return jnp.moveaxis(rel_bias.astype(jnp.float32)[rel_bucket(dist)], -1, 0)


def split_qkv(proj):
    B, S, _ = proj.shape
    q = proj[..., :D_MODEL].reshape(B, S, N_HEADS, HEAD_DIM)
    k = proj[..., D_MODEL:2 * D_MODEL].reshape(B, S, N_HEADS, HEAD_DIM)
    v = proj[..., 2 * D_MODEL:3 * D_MODEL].reshape(B, S, N_HEADS, HEAD_DIM)
    return q, k, v


def dilated_group(q, k, v, rel_bias, window, dil):
    B, S, H, dh = q.shape
    span = window // dil
    L = -(-S // (dil * span)) * span
    pad = L * dil - S
    nb = L // span

    def strided(t):
        t = jnp.pad(t, ((0, 0), (0, pad), (0, 0), (0, 0)))
        return t.reshape(B, nb, span, dil, H, dh)

    def with_prev(t):
        prev = jnp.pad(t[:, :-1], ((0, 0), (1, 0), (0, 0), (0, 0), (0, 0), (0, 0)))
        return jnp.concatenate([prev, t], axis=2)

    qs = strided(q)
    kw, vw = with_prev(strided(k)), with_prev(strided(v))
    qi = jnp.arange(span)[:, None]
    ki = jnp.arange(2 * span)[None, :]
    steps = qi + span - ki
    valid = ((steps >= 0) & (steps <= span)
             & ((jnp.arange(nb)[:, None, None] * span + ki - span) >= 0))
    bias = rel_bias_heads(rel_bias, steps * dil)
    logits = jnp.einsum('bnqrhd,bnkrhd->bnrhqk', qs, kw).astype(jnp.float32) * HEAD_DIM ** -0.5 + bias
    logits = jnp.where(valid[None, :, None, None], logits, NEG)
    m = jnp.max(logits, axis=-1)
    p = jnp.exp(logits - m[..., None])
    den = jnp.sum(p, axis=-1)
    o = jnp.einsum('bnrhqk,bnkrhd->bnqrhd', p, vw.astype(jnp.float32))
    o = o / jnp.moveaxis(den, -1, 2)[..., None]
    lse = jnp.moveaxis(m + jnp.log(den), -1, 2)
    o = o.reshape(B, L * dil, H, dh)[:, :S]
    lse = lse.reshape(B, L * dil, H)[:, :S]
    return o, lse


def dilated_mixer(u, w_in, rel_bias):
    B, S, _ = u.shape
    q, k, v = split_qkv(u @ w_in)
    outs, lses = zip(*[dilated_group(q, k, v, rel_bias, w, d) for (w, d) in DILATED_GROUPS])
    wts = jax.nn.softmax(jnp.stack(lses), axis=0)
    o = jnp.einsum('gbsh,gbshd->bshd', wts, jnp.stack(outs))
    return o.reshape(B, S, D_MODEL).astype(u.dtype)


def moba_mixer(u, w_in, rel_bias):
    B, S, _ = u.shape
    H, dh, BLK, QC = N_HEADS, HEAD_DIM, MOBA_BLOCK, MOBA_Q_CHUNK
    q, k, v = split_qkv(u @ w_in)
    nb = -(-S // BLK)
    sp = nb * BLK
    topk = min(MOBA_TOPK, nb)
    qh = q.transpose(0, 2, 1, 3)
    kp = jnp.pad(k, ((0, 0), (0, sp - S), (0, 0), (0, 0))).transpose(0, 2, 1, 3)
    vp = jnp.pad(v, ((0, 0), (0, sp - S), (0, 0), (0, 0))).transpose(0, 2, 1, 3)
    kb = kp.reshape(B, H, nb, BLK, dh)
    vb = vp.reshape(B, H, nb, BLK, dh)
    kmean = jnp.mean(kb.astype(jnp.float32), axis=3)
    bi = jnp.arange(B)[:, None, None, None]
    hi = jnp.arange(H)[None, :, None, None]
    table = rel_bias.astype(jnp.float32).T
    scale = dh ** -0.5

    def chunk(c):
        start = c * QC
        qc = lax.dynamic_slice_in_dim(qh, start, QC, axis=2).astype(jnp.float32)
        own = start // BLK
        qpos = start + jnp.arange(QC)
        gate = jnp.einsum('bhqd,bhnd->bhqn', qc, kmean)
        gate = jnp.where(jnp.arange(nb) < own, gate, NEG)
        _, idx = lax.top_k(gate, topk)
        slot_ok = jnp.arange(topk) < own
        ksel = kb[bi, hi, idx].astype(jnp.float32)
        vsel = vb[bi, hi, idx].astype(jnp.float32)
        kpos = idx[..., None] * BLK + jnp.arange(BLK)
        s_sel = (jnp.einsum('bhqd,bhqjkd->bhqjk', qc, ksel) * scale
                 + table[hi[..., None], rel_bucket(qpos[:, None, None] - kpos)])
        s_sel = jnp.where(slot_ok[:, None], s_sel, NEG)
        kown = lax.dynamic_slice_in_dim(kp, own * BLK, BLK, axis=2).astype(jnp.float32)
        vown = lax.dynamic_slice_in_dim(vp, own * BLK, BLK, axis=2).astype(jnp.float32)
        dist = qpos[:, None] - (own * BLK + jnp.arange(BLK))[None, :]
        s_own = jnp.einsum('bhqd,bhkd->bhqk', qc, kown) * scale + rel_bias_heads(rel_bias, dist)
        s_own = jnp.where(dist >= 0, s_own, NEG)
        logits = jnp.concatenate([s_sel.reshape(B, H, QC, topk * BLK), s_own], axis=-1)
        p = jax.nn.softmax(logits, axis=-1)
        p_sel = p[..., :topk * BLK].reshape(B, H, QC, topk, BLK)
        p_own = p[..., topk * BLK:]
        return (jnp.einsum('bhqjk,bhqjkd->bhqd', p_sel, vsel)
                + jnp.einsum('bhqk,bhkd->bhqd', p_own, vown))

    o = lax.map(chunk, jnp.arange(S // QC))
    o = o.transpose(1, 0, 3, 2, 4).reshape(B, S, D_MODEL)
    return o.astype(u.dtype)


def diff_mixer(u, w_in, lam, subln, rel_bias, layer_idx):
    B, S, _ = u.shape
    H = N_HEADS
    q, k, v = split_qkv(u @ w_in)
    qh = q.reshape(B, S, H, 2, DIFF_HEAD_DIM).transpose(0, 2, 3, 1, 4).astype(jnp.float32)
    kh = k.reshape(B, S, H, 2, DIFF_HEAD_DIM).transpose(0, 2, 3, 1, 4).astype(jnp.float32)
    vh = v.transpose(0, 2, 1, 3).astype(jnp.float32)
    lambda_init = 0.8 - 0.6 * math.exp(-0.3 * layer_idx)
    lf = lam.astype(jnp.float32)
    lmbda = jnp.exp(jnp.sum(lf[0] * lf[1])) - jnp.exp(jnp.sum(lf[2] * lf[3])) + lambda_init
    kpos = jnp.arange(S)
    scale = DIFF_HEAD_DIM ** -0.5

    def block(c):
        qc = lax.dynamic_slice_in_dim(qh, c * Q_BLOCK, Q_BLOCK, axis=3)
        dist = (c * Q_BLOCK + jnp.arange(Q_BLOCK))[:, None] - kpos[None, :]
        s = jnp.einsum('bhmqd,bhmkd->bhmqk', qc, kh) * scale + rel_bias_heads(rel_bias, dist)[:, None]
        s = jnp.where(dist >= 0, s, NEG)
        a = jax.nn.softmax(s, axis=-1)
        attn = a[:, :, 0] - lmbda * a[:, :, 1]
        return jnp.einsum('bhqk,bhkd->bhqd', attn, vh)

    o = lax.map(block, jnp.arange(S // Q_BLOCK))
    o = o.transpose(1, 0, 3, 2, 4).reshape(B, S, H, HEAD_DIM)
    o = rms_norm(o, subln) * (1.0 - lambda_init)
    return o.reshape(B, S, D_MODEL).astype(u.dtype)


def fox_mixer(u, w_in, b_f):
    B, S, _ = u.shape
    proj = u @ w_in
    q, k, v = split_qkv(proj)
    logf = jax.nn.log_sigmoid((proj[..., 3 * D_MODEL:] + b_f).astype(jnp.float32))
    csum = lax.cumsum(logf, axis=1).transpose(0, 2, 1)
    qh = q.transpose(0, 2, 1, 3).astype(jnp.float32)
    kh = k.transpose(0, 2, 1, 3).astype(jnp.float32)
    vh = v.transpose(0, 2, 1, 3).astype(jnp.float32)
    kpos = jnp.arange(S)
    scale = HEAD_DIM ** -0.5

    def block(c):
        qc = lax.dynamic_slice_in_dim(qh, c * Q_BLOCK, Q_BLOCK, axis=2)
        cq = lax.dynamic_slice_in_dim(csum, c * Q_BLOCK, Q_BLOCK, axis=2)
        dist = (c * Q_BLOCK + jnp.arange(Q_BLOCK))[:, None] - kpos[None, :]
        s = jnp.einsum('bhqd,bhkd->bhqk', qc, kh) * scale + cq[..., None] - csum[:, :, None, :]
        s = jnp.where(dist >= 0, s, NEG)
        return jnp.einsum('bhqk,bhkd->bhqd', jax.nn.softmax(s, axis=-1), vh)

    o = lax.map(block, jnp.arange(S // Q_BLOCK))
    o = o.transpose(1, 0, 3, 2, 4).reshape(B, S, D_MODEL)
    return o.astype(u.dtype)


def conv_glu_ffn(u, w_up, conv_w, conv_b, w_down):
    a = u @ w_up
    a = lax.conv_general_dilated(a, conv_w[:, None, :].astype(a.dtype), window_strides=(1,),
                                 padding=((CONV_WIDTH - 1, 0),),
                                 dimension_numbers=('NWC', 'WIO', 'NWC'),
                                 feature_group_count=a.shape[-1]) + conv_b
    g, up = jnp.split(a, 2, axis=-1)
    return (jax.nn.gelu(g, approximate=True) * up) @ w_down


def setup_inputs(seed: int = 0) -> dict:
    key = jax.random.key(seed)
    ks = jax.random.split(key, 16)
    f32 = jnp.float32
    nA, nB, nC, nD = layers_of(0), layers_of(1), layers_of(2), layers_of(3)

    def dense(k, shape, fan_in):
        return jax.random.normal(k, shape, f32) * fan_in ** -0.5

    return {
        'x': jax.random.normal(ks[0], (BATCH, SEQ, D_MODEL), f32),
        'rel_bias': 0.2 * jax.random.normal(ks[1], (N_BUCKETS, N_HEADS), f32),
        'norm_gains': 1.0 + 0.05 * jax.random.normal(ks[2], (DEPTH, 4, D_MODEL), f32),
        'w_out': dense(ks[3], (DEPTH, D_MODEL, D_MODEL), D_MODEL),
        'w_up': dense(ks[4], (DEPTH, D_MODEL, 2 * D_FF), D_MODEL),
        'conv_w': dense(ks[5], (DEPTH, CONV_WIDTH, 2 * D_FF), CONV_WIDTH),
        'conv_b': 0.02 * jax.random.normal(ks[6], (DEPTH, 2 * D_FF), f32),
        'w_down': dense(ks[7], (DEPTH, D_FF, D_MODEL), D_FF),
        'w_in_dil': dense(ks[8], (nA, D_MODEL, 3 * D_MODEL), D_MODEL),
        'w_in_moba': dense(ks[9], (nB, D_MODEL, 3 * D_MODEL), D_MODEL),
        'w_in_diff': dense(ks[10], (nC, D_MODEL, 3 * D_MODEL), D_MODEL),
        'lambda_diff': 0.1 * jax.random.normal(ks[11], (nC, 4, DIFF_HEAD_DIM), f32),
        'subln_diff': 1.0 + 0.05 * jax.random.normal(ks[12], (nC, HEAD_DIM), f32),
        'w_in_fox': dense(ks[13], (nD, D_MODEL, 3 * D_MODEL + N_HEADS), D_MODEL),
        'b_fox': jax.random.uniform(ks[14], (nD, N_HEADS), f32, minval=1.0, maxval=5.0),
    }


def reference(x, rel_bias, norm_gains, w_out, w_up, conv_w, conv_b, w_down,
              w_in_dil, w_in_moba, w_in_diff, lambda_diff, subln_diff, w_in_fox, b_fox):
    h = x
    for i in range(DEPTH):
        m, j = i % N_MIXERS, i // N_MIXERS
        u = rms_norm(h, norm_gains[i, 0])
        if m == 0:
            y = dilated_mixer(u, w_in_dil[j], rel_bias)
        elif m == 1:
            y = moba_mixer(u, w_in_moba[j], rel_bias)
        elif m == 2:
            y = diff_mixer(u, w_in_diff[j], lambda_diff[j], subln_diff[j], rel_bias, i)
        else:
            y = fox_mixer(u, w_in_fox[j], b_fox[j])
        h = h + rms_norm(y @ w_out[i], norm_gains[i, 1])
        u = rms_norm(h, norm_gains[i, 2])
        h = h + rms_norm(conv_glu_ffn(u, w_up[i], conv_w[i], conv_b[i], w_down[i]), norm_gains[i, 3])
    return h
```

```python
import functools
import math

import jax
import jax.numpy as jnp
import numpy as np
from jax import lax
from jax.experimental import pallas as pl
from jax.experimental.pallas import tpu as pltpu

N_HEADS = 16
HEAD_DIM = 128
N_MIXERS = 4
CONV_WIDTH = 3
RMS_EPS = 1e-6
N_BUCKETS = 32
MAX_DISTANCE = 128
DILATED_GROUPS = ((128, 1), (512, 4), (2048, 16))
MOBA_BLOCK = 256
MOBA_TOPK = 3
DIFF_HEAD_DIM = HEAD_DIM // 2
NEG = -1e30

ATTN_TILE = 256
CONV_HALO = 16
VMEM_LIMIT = 52 * 1024 * 1024

BF16 = jnp.bfloat16
F32 = jnp.float32


def _rms_rows(x, g):
    return x * lax.rsqrt(jnp.mean(x * x, axis=-1, keepdims=True) + RMS_EPS) * g


def _params(sem, vmem=VMEM_LIMIT):
    return pltpu.CompilerParams(dimension_semantics=sem, vmem_limit_bytes=vmem)


def _norm_matmul_kernel(x_ref, g_ref, w_ref, *rest, has_extra):
    if has_extra:
        wx_ref, o_ref, ox_ref, u_ref = rest
    else:
        o_ref, u_ref = rest

    @pl.when(pl.program_id(1) == 0)
    def _():
        u_ref[...] = _rms_rows(x_ref[...], g_ref[...]).astype(BF16)
        if has_extra:
            ox_ref[...] = jnp.dot(u_ref[...], wx_ref[...], preferred_element_type=F32)

    o_ref[...] = jnp.dot(u_ref[...], w_ref[...], preferred_element_type=F32).astype(o_ref.dtype)


def norm_matmul(x, g, w, w_extra=None, *, tm=512, tn=1024):
    m, d = x.shape
    n = w.shape[1]
    has_extra = w_extra is not None
    in_specs = [
        pl.BlockSpec((tm, d), lambda i, j: (i, 0)),
        pl.BlockSpec((1, d), lambda i, j: (0, 0)),
        pl.BlockSpec((d, tn), lambda i, j: (0, j)),
    ]
    out_shape = [jax.ShapeDtypeStruct((m, n), BF16)]
    out_specs = [pl.BlockSpec((tm, tn), lambda i, j: (i, j))]
    args = [x, g.reshape(1, d), w]
    if has_extra:
        nx = w_extra.shape[1]
        in_specs.append(pl.BlockSpec((d, nx), lambda i, j: (0, 0)))
        out_shape.append(jax.ShapeDtypeStruct((m, nx), F32))
        out_specs.append(pl.BlockSpec((tm, nx), lambda i, j: (i, 0)))
        args.append(w_extra)
    outs = pl.pallas_call(
        functools.partial(_norm_matmul_kernel, has_extra=has_extra),
        grid=(m // tm, n // tn),
        in_specs=in_specs,
        out_specs=out_specs,
        out_shape=out_shape,
        scratch_shapes=[pltpu.VMEM((tm, d), BF16)],
        compiler_params=_params(("parallel", "arbitrary")),
        name="norm_matmul_extra" if has_extra else "norm_matmul",
    )(*args)
    return outs if has_extra else outs[0]


def _matmul_norm_res_kernel(y_ref, w_ref, g_ref, res_ref, o_ref, acc_ref, *, nk):
    k = pl.program_id(1)
    part = jnp.dot(y_ref[...], w_ref[...], preferred_element_type=F32)

    @pl.when(k == 0)
    def _():
        acc_ref[...] = part

    @pl.when(k > 0)
    def _():
        acc_ref[...] += part

    @pl.when(k == nk - 1)
    def _():
        o_ref[...] = res_ref[...] + _rms_rows(acc_ref[...], g_ref[...])


def matmul_norm_res(y, w, g, res, *, tm=512, tk=1024):
    m, kdim = y.shape
    d = w.shape[1]
    nk = kdim // tk
    return pl.pallas_call(
        functools.partial(_matmul_norm_res_kernel, nk=nk),
        grid=(m // tm, nk),
        in_specs=[
            pl.BlockSpec((tm, tk), lambda i, k: (i, k)),
            pl.BlockSpec((tk, d), lambda i, k: (k, 0)),
            pl.BlockSpec((1, d), lambda i, k: (0, 0)),
            pl.BlockSpec((tm, d), lambda i, k: (i, 0)),
        ],
        out_specs=pl.BlockSpec((tm, d), lambda i, k: (i, 0)),
        out_shape=jax.ShapeDtypeStruct((m, d), F32),
        scratch_shapes=[pltpu.VMEM((tm, d), F32)],
        compiler_params=_params(("parallel", "arbitrary")),
        name="matmul_norm_res",
    )(y, w, g.reshape(1, d), res)


def _gelu_tanh(x):
    return x * (0.5 * (1.0 + jnp.tanh(math.sqrt(2.0 / math.pi) * (x + 0.044715 * (x * x * x)))))


def _causal_conv3(a, cw, cb):
    r1 = pltpu.roll(a, 1, 0)
    r2 = pltpu.roll(a, 2, 0)
    y = cw[0:1, :] * r2 + cw[1:2, :] * r1 + cw[2:3, :] * a
    return y[CONV_HALO:, :] + cb


def _ffn_up_kernel(x_ref, halo_ref, g_ref, wg_ref, wu_ref, cwg_ref, cwu_ref, cbg_ref, cbu_ref,
                   o_ref, u_ref, *, tm, seq):
    i = pl.program_id(0)

    @pl.when(pl.program_id(1) == 0)
    def _():
        g = g_ref[...]
        u_ref[CONV_HALO:, :] = _rms_rows(x_ref[...], g).astype(BF16)
        keep = jnp.where((i * tm) % seq == 0, 0.0, 1.0)
        u_ref[:CONV_HALO, :] = (_rms_rows(halo_ref[...], g) * keep).astype(BF16)

    u = u_ref[...]
    ag = _causal_conv3(jnp.dot(u, wg_ref[...], preferred_element_type=F32), cwg_ref[...], cbg_ref[...])
    au = _causal_conv3(jnp.dot(u, wu_ref[...], preferred_element_type=F32), cwu_ref[...], cbu_ref[...])
    o_ref[...] = (_gelu_tanh(ag) * au).astype(o_ref.dtype)


def ffn_up(x, g, w_up, conv_w, conv_b, seq, *, tm=512, tn=512):
    m, d = x.shape
    f = w_up.shape[1] // 2
    nn = f // tn
    halo_blocks = tm // CONV_HALO
    cb = conv_b.reshape(1, 2 * f)
    return pl.pallas_call(
        functools.partial(_ffn_up_kernel, tm=tm, seq=seq),
        grid=(m // tm, nn),
        in_specs=[
            pl.BlockSpec((tm, d), lambda i, j: (i, 0)),
            pl.BlockSpec((CONV_HALO, d), lambda i, j: (jnp.maximum(i * halo_blocks - 1, 0), 0)),
            pl.BlockSpec((1, d), lambda i, j: (0, 0)),
            pl.BlockSpec((d, tn), lambda i, j: (0, j)),
            pl.BlockSpec((d, tn), lambda i, j: (0, nn + j)),
            pl.BlockSpec((CONV_WIDTH, tn), lambda i, j: (0, j)),
            pl.BlockSpec((CONV_WIDTH, tn), lambda i, j: (0, nn + j)),
            pl.BlockSpec((1, tn), lambda i, j: (0, j)),
            pl.BlockSpec((1, tn), lambda i, j: (0, nn + j)),
        ],
        out_specs=pl.BlockSpec((tm, tn), lambda i, j: (i, j)),
        out_shape=jax.ShapeDtypeStruct((m, f), BF16),
        scratch_shapes=[pltpu.VMEM((tm + CONV_HALO, d), BF16)],
        compiler_params=_params(("parallel", "arbitrary")),
        name="ffn_up_conv_glu",
    )(x, x, g.reshape(1, d), w_up, w_up, conv_w, conv_w, cb, cb)


def _rel_bucket_np(dist):
    dist = np.maximum(dist, 0)
    max_exact = N_BUCKETS // 2
    d = np.maximum(dist, 1).astype(np.float32)
    large = max_exact + (np.log(d / max_exact) / math.log(MAX_DISTANCE / max_exact)
                         * (N_BUCKETS - max_exact)).astype(np.int32)
    large = np.minimum(large, N_BUCKETS - 1)
    return np.where(dist < max_exact, dist, large)


def _tile_distances(n_delta, t):
    r = np.arange(t)[:, None]
    c = np.arange(t)[None, :]
    return np.stack([dl * t + r - c for dl in range(n_delta)])


def _bias_table(rel_bias, dist, log_mult):
    bias = jnp.take(rel_bias.astype(F32), jnp.asarray(_rel_bucket_np(dist)), axis=0)
    tab = jnp.where(jnp.asarray(np.isfinite(log_mult))[..., None],
                    bias + jnp.asarray(np.where(np.isfinite(log_mult), log_mult, 0.0), F32)[..., None],
                    NEG)
    return jnp.moveaxis(tab, -1, 0)


def _causal_table(rel_bias, n_delta, t):
    dist = _tile_distances(n_delta, t)
    return _bias_table(rel_bias, dist, np.where(dist >= 0, 0.0, -np.inf))


def _dilated_table(rel_bias, n_delta, t):
    dist = _tile_distances(n_delta, t)
    mult = np.zeros(dist.shape, np.float64)
    for window, dil in DILATED_GROUPS:
        mult += (dist >= 0) & (dist <= window) & (dist % dil == 0)
    with np.errstate(divide="ignore"):
        return _bias_table(rel_bias, dist, np.log(mult))


def _softmax_init(m_ref, l_ref, acc_ref):
    m_ref[...] = jnp.full(m_ref.shape, NEG, F32)
    l_ref[...] = jnp.zeros(l_ref.shape, F32)
    acc_ref[...] = jnp.zeros(acc_ref.shape, F32)


def _softmax_step(s, vb, m_ref, l_ref, acc_ref):
    m_prev = m_ref[...]
    m_new = jnp.maximum(m_prev, jnp.max(s, axis=-1, keepdims=True))
    alpha = jnp.exp(m_prev - m_new)
    p = jnp.exp(s - m_new)
    l_ref[...] = alpha * l_ref[...] + jnp.sum(p, axis=-1, keepdims=True)
    acc_ref[...] = alpha * acc_ref[...] + jnp.dot(p.astype(vb.dtype), vb, preferred_element_type=F32)
    m_ref[...] = m_new


def _qk(q, kb):
    return lax.dot_general(q, kb, (((1,), (1,)), ((), ())), preferred_element_type=F32)


def _kv_tile(k_ref, v_ref, j, t):
    off = pl.multiple_of(j * t, t)
    return k_ref[pl.ds(off, t), :], v_ref[pl.ds(off, t), :]


def _table_attn_kernel(q_ref, k_ref, v_ref, tab_ref, o_ref, m_ref, l_ref, acc_ref, *, t, n_delta, scale):
    i = pl.program_id(2)
    _softmax_init(m_ref, l_ref, acc_ref)
    q = q_ref[...]

    def body(j, carry):
        kb, vb = _kv_tile(k_ref, v_ref, j, t)
        s = _qk(q, kb) * scale + tab_ref[jnp.minimum(i - j, n_delta - 1)]
        _softmax_step(s, vb, m_ref, l_ref, acc_ref)
        return carry

    lax.fori_loop(0, i + 1, body, 0)
    o_ref[...] = (acc_ref[...] / l_ref[...]).astype(o_ref.dtype)


def _moba_attn_kernel(q_ref, k_ref, v_ref, tab_ref, o_ref, m_ref, l_ref, acc_ref, kmean_ref,
                      *, t, n_delta, scale, n_blocks):
    i = pl.program_id(2)

    @pl.when(i == 0)
    def _():
        kmean_ref[...] = jnp.zeros(kmean_ref.shape, F32)
        for jb in range(n_blocks):
            kmean_ref[jb:jb + 1, :] = jnp.mean(k_ref[jb * t:(jb + 1) * t, :].astype(F32), axis=0, keepdims=True)

    _softmax_init(m_ref, l_ref, acc_ref)
    q = q_ref[...]
    km = kmean_ref[...]
    km_hi = km.astype(BF16)
    km_lo = (km - km_hi.astype(F32)).astype(BF16)
    gate = _qk(q, km_hi) + _qk(q, km_lo)
    lane = lax.broadcasted_iota(jnp.int32, gate.shape, 1)
    past = lane < i

    kb, vb = _kv_tile(k_ref, v_ref, i, t)
    _softmax_step(_qk(q, kb) * scale + tab_ref[0], vb, m_ref, l_ref, acc_ref)

    def body(j, carry):
        gj = jnp.sum(jnp.where(lane == j, gate, 0.0), axis=-1, keepdims=True)
        beats = past & ((gate > gj) | ((gate == gj) & (lane < j)))
        rank = jnp.sum(jnp.where(beats, 1.0, 0.0), axis=-1, keepdims=True)
        sel = jnp.where(rank < MOBA_TOPK, 0.0, NEG)
        kb, vb = _kv_tile(k_ref, v_ref, j, t)
        s = _qk(q, kb) * scale + tab_ref[jnp.minimum(i - j, n_delta - 1)] + sel
        _softmax_step(s, vb, m_ref, l_ref, acc_ref)
        return carry

    lax.fori_loop(0, i, body, 0)
    o_ref[...] = (acc_ref[...] / l_ref[...]).astype(o_ref.dtype)


def _diff_attn_kernel(q_ref, k_ref, v_ref, tab_ref, lam_ref, sub_ref, o_ref,
                      m1_ref, l1_ref, acc1_ref, m2_ref, l2_ref, acc2_ref,
                      *, t, n_delta, scale, lambda_init):
    i = pl.program_id(2)
    _softmax_init(m1_ref, l1_ref, acc1_ref)
    _softmax_init(m2_ref, l2_ref, acc2_ref)
    q = q_ref[...]
    lane = lax.broadcasted_iota(jnp.int32, q.shape, 1)
    q1 = jnp.where(lane < DIFF_HEAD_DIM, q, jnp.zeros_like(q))
    q2 = jnp.where(lane >= DIFF_HEAD_DIM, q, jnp.zeros_like(q))

    def body(j, carry):
        kb, vb = _kv_tile(k_ref, v_ref, j, t)
        tab = tab_ref[jnp.minimum(i - j, n_delta - 1)]
        _softmax_step(_qk(q1, kb) * scale + tab, vb, m1_ref, l1_ref, acc1_ref)
        _softmax_step(_qk(q2, kb) * scale + tab, vb, m2_ref, l2_ref, acc2_ref)
        return carry

    lax.fori_loop(0, i + 1, body, 0)
    lam = lam_ref[...]
    lmbda = (jnp.exp(jnp.sum(lam[0:1, :] * lam[1:2, :], axis=-1, keepdims=True))
             - jnp.exp(jnp.sum(lam[2:3, :] * lam[3:4, :], axis=-1, keepdims=True)) + lambda_init)
    o = acc1_ref[...] / l1_ref[...] - lmbda * (acc2_ref[...] / l2_ref[...])
    o_ref[...] = (_rms_rows(o, sub_ref[...]) * (1.0 - lambda_init)).astype(o_ref.dtype)


def _fox_attn_kernel(q_ref, k_ref, v_ref, tab_ref, cq_ref, ck_ref, o_ref, m_ref, l_ref, acc_ref,
                     *, t, n_delta, scale):
    h = pl.program_id(1)
    i = pl.program_id(2)
    _softmax_init(m_ref, l_ref, acc_ref)
    q = q_ref[...]
    cs = cq_ref[...]
    lane = lax.broadcasted_iota(jnp.int32, cs.shape, 1)
    cq = jnp.sum(jnp.where(lane == h, cs, 0.0), axis=-1, keepdims=True)

    def body(j, carry):
        kb, vb = _kv_tile(k_ref, v_ref, j, t)
        s = _qk(q, kb) * scale + cq - ck_ref[pl.ds(j, 1), :] + tab_ref[jnp.minimum(i - j, n_delta - 1)]
        _softmax_step(s, vb, m_ref, l_ref, acc_ref)
        return carry

    lax.fori_loop(0, i + 1, body, 0)
    o_ref[...] = (acc_ref[...] / l_ref[...]).astype(o_ref.dtype)


def attention(kind, qkv, tab, batch, seq, extra=()):
    t = ATTN_TILE
    nq = seq // t
    m = qkv.shape[0]
    n_delta = tab.shape[1]
    softmax_scratch = [pltpu.VMEM((t, 1), F32), pltpu.VMEM((t, 1), F32), pltpu.VMEM((t, HEAD_DIM), F32)]
    in_specs = [
        pl.BlockSpec((t, HEAD_DIM), lambda b, h, i: (b * nq + i, h)),
        pl.BlockSpec((seq, HEAD_DIM), lambda b, h, i: (b, N_HEADS + h)),
        pl.BlockSpec((seq, HEAD_DIM), lambda b, h, i: (b, 2 * N_HEADS + h)),
        pl.BlockSpec((None, n_delta, t, t),
                     (lambda b, h, i: (h, 0, 0, 0)) if tab.shape[0] == N_HEADS else (lambda b, h, i: (0, 0, 0, 0))),
    ]
    args = [qkv, qkv, qkv, tab]
    scratch = list(softmax_scratch)
    common = dict(t=t, n_delta=n_delta)
    if kind == "dilated":
        body = functools.partial(_table_attn_kernel, scale=HEAD_DIM ** -0.5, **common)
    elif kind == "moba":
        body = functools.partial(_moba_attn_kernel, scale=HEAD_DIM ** -0.5, n_blocks=seq // t, **common)
        scratch.append(pltpu.VMEM((HEAD_DIM, HEAD_DIM), F32))
    elif kind == "diff":
        lam, subln, lambda_init = extra
        body = functools.partial(_diff_attn_kernel, scale=DIFF_HEAD_DIM ** -0.5,
                                 lambda_init=lambda_init, **common)
        in_specs += [pl.BlockSpec(lam.shape, lambda b, h, i: (0, 0)),
                     pl.BlockSpec((1, HEAD_DIM), lambda b, h, i: (0, 0))]
        args += [lam, subln.reshape(1, HEAD_DIM)]
        scratch += softmax_scratch
    elif kind == "fox":
        csum, csum_t = extra
        body = functools.partial(_fox_attn_kernel, scale=HEAD_DIM ** -0.5, **common)
        in_specs += [pl.BlockSpec((None, t, HEAD_DIM), lambda b, h, i: (b, i, 0)),
                     pl.BlockSpec((None, None, nq, t), lambda b, h, i: (b, h, 0, 0))]
        args += [csum, csum_t]
    else:
        raise ValueError(kind)
    return pl.pallas_call(
        body,
        grid=(batch, N_HEADS, nq),
        in_specs=in_specs,
        out_specs=pl.BlockSpec((t, HEAD_DIM), lambda b, h, i: (b * nq + i, h)),
        out_shape=jax.ShapeDtypeStruct((m, N_HEADS * HEAD_DIM), BF16),
        scratch_shapes=scratch,
        compiler_params=_params(("parallel", "parallel", "arbitrary")),
        name=kind + "_attention",
    )(*args)


def _fox_gate_kernel(f_ref, b_ref, cs_ref, cst_ref):
    z = f_ref[...] + b_ref[...]
    x = jnp.minimum(z, 0.0) - jnp.log1p(jnp.exp(-jnp.abs(z)))
    row = lax.broadcasted_iota(jnp.int32, x.shape, 0)
    shift = 1
    while shift < x.shape[0]:
        x = x + jnp.where(row >= shift, pltpu.roll(x, shift, 0), 0.0)
        shift *= 2
    cs_ref[...] = x
    cst_ref[...] = x.T


def fox_gate(f, b_pad):
    batch, seq, lanes = f.shape
    return pl.pallas_call(
        _fox_gate_kernel,
        grid=(batch,),
        in_specs=[pl.BlockSpec((None, seq, lanes), lambda b: (b, 0, 0)),
                  pl.BlockSpec((1, lanes), lambda b: (0, 0))],
        out_specs=[pl.BlockSpec((None, seq, lanes), lambda b: (b, 0, 0)),
                   pl.BlockSpec((None, lanes, seq), lambda b: (b, 0, 0))],
        out_shape=[jax.ShapeDtypeStruct((batch, seq, lanes), F32),
                   jax.ShapeDtypeStruct((batch, lanes, seq), F32)],
        compiler_params=_params(("parallel",)),
        name="fox_gate_cumsum",
    )(f, b_pad)


def kernel(x, rel_bias, norm_gains, w_out, w_up, conv_w, conv_b, w_down, w_in_dil, w_in_moba,
           w_in_diff, lambda_diff, subln_diff, w_in_fox, b_fox):
    batch, seq, d = x.shape
    depth = norm_gains.shape[0]
    t = ATTN_TILE
    assert seq % t == 0 and t == MOBA_BLOCK and d == N_HEADS * HEAD_DIM
    h = x.reshape(batch * seq, d)
    for i in range(depth):
        mixer, j = i % N_MIXERS, i // N_MIXERS
        g = norm_gains[i]
        if mixer == 0:
            qkv = norm_matmul(h, g[0], w_in_dil[j].astype(BF16))
            y = attention("dilated", qkv, _dilated_table(rel_bias, 4, t), batch, seq)
        elif mixer == 1:
            qkv = norm_matmul(h, g[0], w_in_moba[j].astype(BF16))
            y = attention("moba", qkv, _causal_table(rel_bias, 3, t), batch, seq)
        elif mixer == 2:
            qkv = norm_matmul(h, g[0], w_in_diff[j].astype(BF16))
            lambda_init = 0.8 - 0.6 * math.exp(-0.3 * i)
            y = attention("diff", qkv, _causal_table(rel_bias, 3, t), batch, seq,
                          extra=(lambda_diff[j], subln_diff[j], lambda_init))
        else:
            w = w_in_fox[j]
            w_f = jnp.pad(w[:, 3 * d:], ((0, 0), (0, HEAD_DIM - N_HEADS))).astype(BF16)
            qkv, f = norm_matmul(h, g[0], w[:, :3 * d].astype(BF16), w_f)
            b_pad = jnp.pad(b_fox[j], (0, HEAD_DIM - N_HEADS)).reshape(1, HEAD_DIM)
            csum, csum_t = fox_gate(f.reshape(batch, seq, HEAD_DIM), b_pad)
            causal = jnp.asarray(np.where(_tile_distances(2, t) >= 0, 0.0, NEG)[None], F32)
            y = attention("fox", qkv, causal, batch, seq,
                          extra=(csum, csum_t.reshape(batch, HEAD_DIM, seq // t, t)))
        h = matmul_norm_res(y, w_out[i].astype(BF16), g[1], h, tm=256, tk=d)
        hidden = ffn_up(h, g[2], w_up[i].astype(BF16), conv_w[i], conv_b[i], seq)
        h = matmul_norm_res(hidden, w_down[i].astype(BF16), g[3], h)
    return h.reshape(batch, seq, d)
```

```python
import functools
import math

import jax
import jax.numpy as jnp
import numpy as np
from jax import lax
from jax.experimental import pallas as pl
from jax.experimental.pallas import tpu as pltpu

N_HEADS = 16
HEAD_DIM = 128
N_MIXERS = 4
CONV_WIDTH = 3
RMS_EPS = 1e-6
N_BUCKETS = 32
MAX_DISTANCE = 128
DILATED_GROUPS = ((128, 1), (512, 4), (2048, 16))
MOBA_BLOCK = 256
MOBA_TOPK = 3
DIFF_HEAD_DIM = HEAD_DIM // 2
NEG = -1e30

ATTN_TILE = 256
ATTN_HEADS_PER_STEP = 2
CONV_HALO = 16
VMEM_LIMIT = 52 * 1024 * 1024

BF16 = jnp.bfloat16
F32 = jnp.float32


def _rms_rows(x, g):
    return x * lax.rsqrt(jnp.mean(x * x, axis=-1, keepdims=True) + RMS_EPS) * g


def _params(sem, vmem=VMEM_LIMIT):
    return pltpu.CompilerParams(dimension_semantics=sem, vmem_limit_bytes=vmem)


def _norm_matmul_kernel(x_ref, g_ref, w_ref, *rest, has_extra):
    if has_extra:
        wx_ref, o_ref, ox_ref, u_ref = rest
    else:
        o_ref, u_ref = rest

    @pl.when(pl.program_id(1) == 0)
    def _():
        u_ref[...] = _rms_rows(x_ref[...], g_ref[...]).astype(BF16)
        if has_extra:
            ox_ref[...] = jnp.dot(u_ref[...], wx_ref[...], preferred_element_type=F32)

    o_ref[...] = jnp.dot(u_ref[...], w_ref[...], preferred_element_type=F32).astype(o_ref.dtype)


def norm_matmul(x, g, w, w_extra=None, *, tm=512, tn=1024):
    m, d = x.shape
    n = w.shape[1]
    has_extra = w_extra is not None
    in_specs = [
        pl.BlockSpec((tm, d), lambda i, j: (i, 0)),
        pl.BlockSpec((1, d), lambda i, j: (0, 0)),
        pl.BlockSpec((d, tn), lambda i, j: (0, j)),
    ]
    out_shape = [jax.ShapeDtypeStruct((m, n), BF16)]
    out_specs = [pl.BlockSpec((tm, tn), lambda i, j: (i, j))]
    args = [x, g.reshape(1, d), w]
    if has_extra:
        nx = w_extra.shape[1]
        in_specs.append(pl.BlockSpec((d, nx), lambda i, j: (0, 0)))
        out_shape.append(jax.ShapeDtypeStruct((m, nx), F32))
        out_specs.append(pl.BlockSpec((tm, nx), lambda i, j: (i, 0)))
        args.append(w_extra)
    outs = pl.pallas_call(
        functools.partial(_norm_matmul_kernel, has_extra=has_extra),
        grid=(m // tm, n // tn),
        in_specs=in_specs,
        out_specs=out_specs,
        out_shape=out_shape,
        scratch_shapes=[pltpu.VMEM((tm, d), BF16)],
        compiler_params=_params(("parallel", "arbitrary")),
        name="norm_matmul_extra" if has_extra else "norm_matmul",
    )(*args)
    return outs if has_extra else outs[0]


def _matmul_norm_res_kernel(y_ref, w_ref, g_ref, res_ref, o_ref, acc_ref, *, nk):
    k = pl.program_id(1)
    part = jnp.dot(y_ref[...], w_ref[...], preferred_element_type=F32)

    @pl.when(k == 0)
    def _():
        acc_ref[...] = part

    @pl.when(k > 0)
    def _():
        acc_ref[...] += part

    @pl.when(k == nk - 1)
    def _():
        o_ref[...] = res_ref[...] + _rms_rows(acc_ref[...], g_ref[...])


def matmul_norm_res(y, w, g, res, *, tm=512, tk=1024):
    m, kdim = y.shape
    d = w.shape[1]
    nk = kdim // tk
    return pl.pallas_call(
        functools.partial(_matmul_norm_res_kernel, nk=nk),
        grid=(m // tm, nk),
        in_specs=[
            pl.BlockSpec((tm, tk), lambda i, k: (i, k)),
            pl.BlockSpec((tk, d), lambda i, k: (k, 0)),
            pl.BlockSpec((1, d), lambda i, k: (0, 0)),
            pl.BlockSpec((tm, d), lambda i, k: (i, 0)),
        ],
        out_specs=pl.BlockSpec((tm, d), lambda i, k: (i, 0)),
        out_shape=jax.ShapeDtypeStruct((m, d), F32),
        scratch_shapes=[pltpu.VMEM((tm, d), F32)],
        compiler_params=_params(("parallel", "arbitrary")),
        name="matmul_norm_res",
    )(y, w, g.reshape(1, d), res)


def _gelu_tanh(x):
    return x * (0.5 * (1.0 + jnp.tanh(math.sqrt(2.0 / math.pi) * (x + 0.044715 * (x * x * x)))))


def _causal_conv3(a, cw, cb):
    r1 = pltpu.roll(a, 1, 0)
    r2 = pltpu.roll(a, 2, 0)
    y = cw[0:1, :] * r2 + cw[1:2, :] * r1 + cw[2:3, :] * a
    return y[CONV_HALO:, :] + cb


def _ffn_up_kernel(x_ref, halo_ref, g_ref, wg_ref, wu_ref, cwg_ref, cwu_ref, cbg_ref, cbu_ref,
                   o_ref, u_ref, *, tm, seq):
    i = pl.program_id(0)

    @pl.when(pl.program_id(1) == 0)
    def _():
        g = g_ref[...]
        u_ref[CONV_HALO:, :] = _rms_rows(x_ref[...], g).astype(BF16)
        keep = jnp.where((i * tm) % seq == 0, 0.0, 1.0)
        u_ref[:CONV_HALO, :] = (_rms_rows(halo_ref[...], g) * keep).astype(BF16)

    u = u_ref[...]
    ag = _causal_conv3(jnp.dot(u, wg_ref[...], preferred_element_type=F32), cwg_ref[...], cbg_ref[...])
    au = _causal_conv3(jnp.dot(u, wu_ref[...], preferred_element_type=F32), cwu_ref[...], cbu_ref[...])
    o_ref[...] = (_gelu_tanh(ag) * au).astype(o_ref.dtype)


def ffn_up(x, g, w_up, conv_w, conv_b, seq, *, tm=512, tn=512):
    m, d = x.shape
    f = w_up.shape[1] // 2
    nn = f // tn
    halo_blocks = tm // CONV_HALO
    cb = conv_b.reshape(1, 2 * f)
    return pl.pallas_call(
        functools.partial(_ffn_up_kernel, tm=tm, seq=seq),
        grid=(m // tm, nn),
        in_specs=[
            pl.BlockSpec((tm, d), lambda i, j: (i, 0)),
            pl.BlockSpec((CONV_HALO, d), lambda i, j: (jnp.maximum(i * halo_blocks - 1, 0), 0)),
            pl.BlockSpec((1, d), lambda i, j: (0, 0)),
            pl.BlockSpec((d, tn), lambda i, j: (0, j)),
            pl.BlockSpec((d, tn), lambda i, j: (0, nn + j)),
            pl.BlockSpec((CONV_WIDTH, tn), lambda i, j: (0, j)),
            pl.BlockSpec((CONV_WIDTH, tn), lambda i, j: (0, nn + j)),
            pl.BlockSpec((1, tn), lambda i, j: (0, j)),
            pl.BlockSpec((1, tn), lambda i, j: (0, nn + j)),
        ],
        out_specs=pl.BlockSpec((tm, tn), lambda i, j: (i, j)),
        out_shape=jax.ShapeDtypeStruct((m, f), BF16),
        scratch_shapes=[pltpu.VMEM((tm + CONV_HALO, d), BF16)],
        compiler_params=_params(("parallel", "arbitrary")),
        name="ffn_up_conv_glu",
    )(x, x, g.reshape(1, d), w_up, w_up, conv_w, conv_w, cb, cb)


def _rel_bucket_np(dist):
    dist = np.maximum(dist, 0)
    max_exact = N_BUCKETS // 2
    d = np.maximum(dist, 1).astype(np.float32)
    large = max_exact + (np.log(d / max_exact) / math.log(MAX_DISTANCE / max_exact)
                         * (N_BUCKETS - max_exact)).astype(np.int32)
    large = np.minimum(large, N_BUCKETS - 1)
    return np.where(dist < max_exact, dist, large)


def _strip_distances(n_delta, t):
    return np.arange(n_delta)[:, None] * t + np.arange(2 * t)[None, :] - t


def _causal_log_mult(dist):
    return np.where(dist >= 0, 0.0, -np.inf)


def _dilated_log_mult(dist):
    mult = np.zeros(dist.shape, np.float64)
    for window, dil in DILATED_GROUPS:
        mult += (dist >= 0) & (dist <= window) & (dist % dil == 0)
    with np.errstate(divide="ignore"):
        return np.log(mult)


def _bias_strips(rel_bias, dist, log_mult):
    shown = np.isfinite(log_mult)
    offset = jnp.asarray(np.where(shown, log_mult, NEG), F32)
    if rel_bias is None:
        return offset[None]
    bias = jnp.take(rel_bias.astype(F32), jnp.asarray(_rel_bucket_np(dist)), axis=0)
    return jnp.moveaxis(jnp.where(jnp.asarray(shown)[..., None], bias + offset[..., None], NEG), -1, 0)


def _softmax_step(s, vtb, stat, idx):
    m_ref, l_ref, acc_ref = stat
    m_prev = m_ref[idx]
    m_new = jnp.maximum(m_prev, jnp.max(s, axis=0, keepdims=True))
    alpha = jnp.exp(m_prev - m_new)
    p = jnp.exp(s - m_new)
    l_ref[idx] = alpha * l_ref[idx] + jnp.sum(p, axis=0, keepdims=True)
    acc_ref[idx] = alpha * acc_ref[idx] + jnp.dot(vtb, p.astype(BF16), preferred_element_type=F32)
    m_ref[idx] = m_new


def _attn_kernel(*refs, kind, g, t, seq, n_delta, scale, lambda_init):
    n_in = {"dilated": 4, "moba": 4, "diff": 6, "fox": 6}[kind]
    q_ref, k_ref, v_ref, strip_ref = refs[:4]
    o_ref = refs[n_in]
    tab_ref, vt_ref, m_ref, l_ref, acc_ref = refs[n_in + 1:n_in + 6]
    stat = (m_ref, l_ref, acc_ref)
    hg = pl.program_id(1)
    i = pl.program_id(2)
    nk = seq // t
    n_tab = tab_ref.shape[0]
    dh = HEAD_DIM
    heads = [slice(hh * dh, (hh + 1) * dh) for hh in range(g)]

    @pl.when(i == 0)
    def _():
        for hh in range(n_tab):
            for dl in range(n_delta):
                strip = jnp.broadcast_to(strip_ref[hh, dl:dl + 1, :], (t, 2 * t))
                tab_ref[hh, dl] = pltpu.roll(strip, 0, 1, stride=1, stride_axis=0)[:, t:]
        for jb in range(nk):
            vt_ref[jb] = v_ref[jb * t:(jb + 1) * t, :].astype(F32).T.astype(BF16)
        if kind == "moba":
            kmean_ref = refs[n_in + 6]
            for hh in range(g):
                for jb in range(nk):
                    kmean_ref[hh, jb:jb + 1, :] = jnp.mean(
                        k_ref[jb * t:(jb + 1) * t, heads[hh]].astype(F32), axis=0, keepdims=True)
        if kind == "fox":
            cs_ref, ckb_ref = refs[4], refs[n_in + 6]
            cs = cs_ref[...]
            lane = lax.broadcasted_iota(jnp.int32, cs.shape, 1)
            for hh in range(g):
                ck = jnp.sum(jnp.where(lane == hg * g + hh, cs, 0.0), axis=-1, keepdims=True)
                ckb_ref[hh] = jnp.broadcast_to(ck, cs.shape)

    n_maps = 2 if kind == "diff" else 1
    for idx in range(g * n_maps):
        m_ref[idx] = jnp.full(m_ref.shape[1:], NEG, F32)
        l_ref[idx] = jnp.zeros(l_ref.shape[1:], F32)
        acc_ref[idx] = jnp.zeros(acc_ref.shape[1:], F32)

    qt_all = q_ref[...].astype(F32).T.astype(BF16)
    qt = [qt_all[heads[hh], :] for hh in range(g)]
    if kind == "diff":
        row = lax.broadcasted_iota(jnp.int32, qt[0].shape, 0)
        qt = [jnp.where(row < DIFF_HEAD_DIM if mp == 0 else row >= DIFF_HEAD_DIM, qh, jnp.zeros_like(qh))
              for qh in qt for mp in range(2)]

    def scores(idx, j, dl):
        hh = idx // n_maps
        off = pl.multiple_of(j * t, t)
        kb = k_ref[pl.ds(off, t), heads[hh]]
        s = jnp.dot(kb, qt[idx], preferred_element_type=F32) * scale
        if kind == "fox":
            cst_ref, ckb_ref = refs[5], refs[n_in + 6]
            ck = ckb_ref[hh, pl.ds(off, t), :]
            s = s + cst_ref[hh, pl.ds(i, 1), :] - jnp.concatenate([ck] * (t // dh), axis=1)
        return s + tab_ref[hh % n_tab, dl]

    def step(idx, j, s):
        _softmax_step(s, vt_ref[j, heads[idx // n_maps], :], stat, idx)

    if kind == "moba":
        kmean_ref, sel_ref = refs[n_in + 6], refs[n_in + 7]
        for hh in range(g):
            km = kmean_ref[hh]
            km_hi = km.astype(BF16)
            km_lo = (km - km_hi.astype(F32)).astype(BF16)
            gate = (jnp.dot(km_hi, qt[hh], preferred_element_type=F32)
                    + jnp.dot(km_lo, qt[hh], preferred_element_type=F32))
            blk = lax.broadcasted_iota(jnp.int32, gate.shape, 0)
            for jb in range(nk):
                gj = gate[jb:jb + 1, :]
                beats = (blk < i) & ((gate > gj) | ((gate == gj) & (blk < jb)))
                rank = jnp.sum(jnp.where(beats, 1.0, 0.0), axis=0, keepdims=True)
                sel_ref[hh, jb:jb + 1, :] = jnp.where(rank < MOBA_TOPK, 0.0, NEG)
            step(hh, i, scores(hh, i, 0))

        def body(j, carry):
            dl = jnp.minimum(i - j, n_delta - 1)
            for hh in range(g):
                step(hh, j, scores(hh, j, dl) + sel_ref[hh, pl.ds(j, 1), :])
            return carry

        lax.fori_loop(0, i, body, 0)
    else:
        def body(j, carry):
            dl = jnp.minimum(i - j, n_delta - 1)
            for idx in range(g * n_maps):
                step(idx, j, scores(idx, j, dl))
            return carry

        lax.fori_loop(0, i + 1, body, 0)

    for hh in range(g):
        if kind == "diff":
            lam_ref, sub_ref = refs[4], refs[5]
            lam = lam_ref[...]
            lmbda = (jnp.exp(jnp.sum(lam[0:1, :] * lam[1:2, :], axis=-1, keepdims=True))
                     - jnp.exp(jnp.sum(lam[2:3, :] * lam[3:4, :], axis=-1, keepdims=True)) + lambda_init)
            ot = acc_ref[2 * hh] / l_ref[2 * hh] - lmbda * (acc_ref[2 * hh + 1] / l_ref[2 * hh + 1])
            out = _rms_rows(ot.T, sub_ref[...]) * (1.0 - lambda_init)
        else:
            out = (acc_ref[hh] / l_ref[hh]).T
        o_ref[:, heads[hh]] = out.astype(o_ref.dtype)


def attention(kind, qkv, strips, batch, seq, extra=(), lambda_init=0.0):
    t, g, dh = ATTN_TILE, ATTN_HEADS_PER_STEP, HEAD_DIM
    nq = seq // t
    ng = N_HEADS // g
    m = qkv.shape[0]
    n_tab = g if strips.shape[0] == N_HEADS else 1
    n_delta = strips.shape[1]
    n_maps = 2 if kind == "diff" else 1
    in_specs = [
        pl.BlockSpec((t, g * dh), lambda b, h, i: (b * nq + i, h)),
        pl.BlockSpec((seq, g * dh), lambda b, h, i: (b, ng + h)),
        pl.BlockSpec((seq, g * dh), lambda b, h, i: (b, 2 * ng + h)),
        pl.BlockSpec((n_tab, n_delta, 2 * t),
                     (lambda b, h, i: (h, 0, 0)) if n_tab == g else (lambda b, h, i: (0, 0, 0))),
    ]
    args = [qkv, qkv, qkv, strips]
    scratch = [
        pltpu.VMEM((n_tab, n_delta, t, t), F32),
        pltpu.VMEM((seq // t, g * dh, t), BF16),
        pltpu.VMEM((g * n_maps, 1, t), F32),
        pltpu.VMEM((g * n_maps, 1, t), F32),
        pltpu.VMEM((g * n_maps, dh, t), F32),
    ]
    if kind == "moba":
        scratch += [pltpu.VMEM((g, seq // t, dh), F32), pltpu.VMEM((g, seq // t, t), F32)]
    elif kind == "diff":
        lam, subln = extra
        in_specs += [pl.BlockSpec(lam.shape, lambda b, h, i: (0, 0)),
                     pl.BlockSpec((1, dh), lambda b, h, i: (0, 0))]
        args += [lam, subln.reshape(1, dh)]
    elif kind == "fox":
        csum, csum_t = extra
        in_specs += [pl.BlockSpec((None, seq, dh), lambda b, h, i: (b, 0, 0)),
                     pl.BlockSpec((None, g, nq, t), lambda b, h, i: (b, h, 0, 0))]
        args += [csum, csum_t]
        scratch.append(pltpu.VMEM((g, seq, dh), F32))
    scale = (DIFF_HEAD_DIM if kind == "diff" else HEAD_DIM) ** -0.5
    return pl.pallas_call(
        functools.partial(_attn_kernel, kind=kind, g=g, t=t, seq=seq, n_delta=n_delta, scale=scale,
                          lambda_init=lambda_init),
        grid=(batch, ng, nq),
        in_specs=in_specs,
        out_specs=pl.BlockSpec((t, g * dh), lambda b, h, i: (b * nq + i, h)),
        out_shape=jax.ShapeDtypeStruct((m, N_HEADS * dh), BF16),
        scratch_shapes=scratch,
        compiler_params=_params(("parallel", "parallel", "arbitrary")),
        name=kind + "_attention",
    )(*args)


def _fox_gate_kernel(f_ref, b_ref, cs_ref, cst_ref):
    z = f_ref[...] + b_ref[...]
    x = jnp.minimum(z, 0.0) - jnp.log1p(jnp.exp(-jnp.abs(z)))
    row = lax.broadcasted_iota(jnp.int32, x.shape, 0)
    shift = 1
    while shift < x.shape[0]:
        x = x + jnp.where(row >= shift, pltpu.roll(x, shift, 0), 0.0)
        shift *= 2
    cs_ref[...] = x
    cst_ref[...] = x.T


def fox_gate(f, b_pad):
    batch, seq, lanes = f.shape
    return pl.pallas_call(
        _fox_gate_kernel,
        grid=(batch,),
        in_specs=[pl.BlockSpec((None, seq, lanes), lambda b: (b, 0, 0)),
                  pl.BlockSpec((1, lanes), lambda b: (0, 0))],
        out_specs=[pl.BlockSpec((None, seq, lanes), lambda b: (b, 0, 0)),
                   pl.BlockSpec((None, lanes, seq), lambda b: (b, 0, 0))],
        out_shape=[jax.ShapeDtypeStruct((batch, seq, lanes), F32),
                   jax.ShapeDtypeStruct((batch, lanes, seq), F32)],
        compiler_params=_params(("parallel",)),
        name="fox_gate_cumsum",
    )(f, b_pad)


def kernel(x, rel_bias, norm_gains, w_out, w_up, conv_w, conv_b, w_down, w_in_dil, w_in_moba,
           w_in_diff, lambda_diff, subln_diff, w_in_fox, b_fox):
    batch, seq, d = x.shape
    depth = norm_gains.shape[0]
    t = ATTN_TILE
    assert seq % t == 0 and t == MOBA_BLOCK and d == N_HEADS * HEAD_DIM
    causal3 = _strip_distances(3, t)
    causal_strips = _bias_strips(rel_bias, causal3, _causal_log_mult(causal3))
    h = x.reshape(batch * seq, d)
    for i in range(depth):
        mixer, j = i % N_MIXERS, i // N_MIXERS
        g = norm_gains[i]
        if mixer == 0:
            qkv = norm_matmul(h, g[0], w_in_dil[j].astype(BF16))
            dist = _strip_distances(4, t)
            y = attention("dilated", qkv, _bias_strips(rel_bias, dist, _dilated_log_mult(dist)), batch, seq)
        elif mixer == 1:
            qkv = norm_matmul(h, g[0], w_in_moba[j].astype(BF16))
            y = attention("moba", qkv, causal_strips, batch, seq)
        elif mixer == 2:
            qkv = norm_matmul(h, g[0], w_in_diff[j].astype(BF16))
            y = attention("diff", qkv, causal_strips, batch, seq, extra=(lambda_diff[j], subln_diff[j]),
                          lambda_init=0.8 - 0.6 * math.exp(-0.3 * i))
        else:
            w = w_in_fox[j]
            w_f = jnp.pad(w[:, 3 * d:], ((0, 0), (0, HEAD_DIM - N_HEADS))).astype(BF16)
            qkv, f = norm_matmul(h, g[0], w[:, :3 * d].astype(BF16), w_f)
            b_pad = jnp.pad(b_fox[j], (0, HEAD_DIM - N_HEADS)).reshape(1, HEAD_DIM)
            csum, csum_t = fox_gate(f.reshape(batch, seq, HEAD_DIM), b_pad)
            dist = _strip_distances(2, t)
            y = attention("fox", qkv, _bias_strips(None, dist, _causal_log_mult(dist)), batch, seq,
                          extra=(csum, csum_t.reshape(batch, HEAD_DIM, seq // t, t)))
        h = matmul_norm_res(y, w_out[i].astype(BF16), g[1], h, tm=256, tk=d)
        hidden = ffn_up(h, g[2], w_up[i].astype(BF16), conv_w[i], conv_b[i], seq)
        h = matmul_norm_res(hidden, w_down[i].astype(BF16), g[3], h)
    return h.reshape(batch, seq, d)
```

```python
import functools
import math

import jax
import jax.numpy as jnp
import numpy as np
from jax import lax
from jax.experimental import pallas as pl
from jax.experimental.pallas import tpu as pltpu

N_HEADS = 16
HEAD_DIM = 128
N_MIXERS = 4
CONV_WIDTH = 3
RMS_EPS = 1e-6
N_BUCKETS = 32
MAX_DISTANCE = 128
DILATED_GROUPS = ((128, 1), (512, 4), (2048, 16))
MOBA_BLOCK = 256
MOBA_TOPK = 3
DIFF_HEAD_DIM = HEAD_DIM // 2
NEG = -1e30
LOG2E = math.log2(math.e)

ATTN_TILE = 256
ATTN_HEADS_PER_STEP = 1
CONV_HALO = 16
VMEM_LIMIT = 52 * 1024 * 1024

BF16 = jnp.bfloat16
F32 = jnp.float32


def _rms_rows(x, g):
    return x * lax.rsqrt(jnp.mean(x * x, axis=-1, keepdims=True) + RMS_EPS) * g


def _params(sem, vmem=VMEM_LIMIT):
    return pltpu.CompilerParams(dimension_semantics=sem, vmem_limit_bytes=vmem)


def _norm_matmul_kernel(x_ref, g_ref, w_ref, cs_ref, *rest, has_extra):
    if has_extra:
        wx_ref, o_ref, ox_ref, u_ref = rest
    else:
        o_ref, u_ref = rest

    @pl.when(pl.program_id(1) == 0)
    def _():
        u_ref[...] = _rms_rows(x_ref[...], g_ref[...]).astype(BF16)
        if has_extra:
            ox_ref[...] = jnp.dot(u_ref[...], wx_ref[...], preferred_element_type=F32)

    acc = jnp.dot(u_ref[...], w_ref[...], preferred_element_type=F32)
    o_ref[...] = (acc * cs_ref[...]).astype(o_ref.dtype)


def norm_matmul(x, g, w, col_scale, w_extra=None, *, tm=1024, tn=1024):
    m, d = x.shape
    n = w.shape[1]
    has_extra = w_extra is not None
    in_specs = [
        pl.BlockSpec((tm, d), lambda i, j: (i, 0)),
        pl.BlockSpec((1, d), lambda i, j: (0, 0)),
        pl.BlockSpec((d, tn), lambda i, j: (0, j)),
        pl.BlockSpec((1, tn), lambda i, j: (0, j)),
    ]
    out_shape = [jax.ShapeDtypeStruct((m, n), BF16)]
    out_specs = [pl.BlockSpec((tm, tn), lambda i, j: (i, j))]
    args = [x, g.reshape(1, d), w, col_scale.reshape(1, n)]
    if has_extra:
        nx = w_extra.shape[1]
        in_specs.append(pl.BlockSpec((d, nx), lambda i, j: (0, 0)))
        out_shape.append(jax.ShapeDtypeStruct((m, nx), F32))
        out_specs.append(pl.BlockSpec((tm, nx), lambda i, j: (i, 0)))
        args.append(w_extra)
    outs = pl.pallas_call(
        functools.partial(_norm_matmul_kernel, has_extra=has_extra),
        grid=(m // tm, n // tn),
        in_specs=in_specs,
        out_specs=out_specs,
        out_shape=out_shape,
        scratch_shapes=[pltpu.VMEM((tm, d), BF16)],
        compiler_params=_params(("parallel", "arbitrary")),
        name="norm_matmul_extra" if has_extra else "norm_matmul",
    )(*args)
    return outs if has_extra else outs[0]


def _matmul_norm_res_kernel(y_ref, w_ref, g_ref, res_ref, o_ref, *, nk):
    k = pl.program_id(1)
    if nk == 1:
        o_ref[...] = res_ref[...] + _rms_rows(jnp.dot(y_ref[...], w_ref[...], preferred_element_type=F32),
                                              g_ref[...])
        return

    @pl.when(k == 0)
    def _():
        o_ref[...] = jnp.dot(y_ref[...], w_ref[...], preferred_element_type=F32)

    @pl.when(k > 0)
    def _():
        o_ref[...] += jnp.dot(y_ref[...], w_ref[...], preferred_element_type=F32)

    @pl.when(k == nk - 1)
    def _():
        o_ref[...] = res_ref[...] + _rms_rows(o_ref[...], g_ref[...])


def matmul_norm_res(y, w, g, res, *, tm=512, tk=2048):
    m, kdim = y.shape
    d = w.shape[1]
    nk = kdim // tk
    return pl.pallas_call(
        functools.partial(_matmul_norm_res_kernel, nk=nk),
        grid=(m // tm, nk),
        in_specs=[
            pl.BlockSpec((tm, tk), lambda i, k: (i, k)),
            pl.BlockSpec((tk, d), lambda i, k: (k, 0)),
            pl.BlockSpec((1, d), lambda i, k: (0, 0)),
            pl.BlockSpec((tm, d), lambda i, k: (i, 0)),
        ],
        out_specs=pl.BlockSpec((tm, d), lambda i, k: (i, 0)),
        out_shape=jax.ShapeDtypeStruct((m, d), F32),
        compiler_params=_params(("parallel", "arbitrary")),
        name="matmul_norm_res",
    )(y, w, g.reshape(1, d), res)


def _gelu_tanh(x):
    return x * (0.5 * (1.0 + jnp.tanh(math.sqrt(2.0 / math.pi) * (x + 0.044715 * (x * x * x)))))


def _causal_conv3(a, cw, cb):
    r1 = pltpu.roll(a, 1, 0)
    r2 = pltpu.roll(a, 2, 0)
    y = cw[0:1, :] * r2 + cw[1:2, :] * r1 + cw[2:3, :] * a
    return y[CONV_HALO:, :] + cb


def _ffn_up_kernel(x_ref, halo_ref, g_ref, wg_ref, wu_ref, cwg_ref, cwu_ref, cbg_ref, cbu_ref,
                   o_ref, u_ref, *, tm, seq):
    i = pl.program_id(0)

    @pl.when(pl.program_id(1) == 0)
    def _():
        g = g_ref[...]
        u_ref[CONV_HALO:, :] = _rms_rows(x_ref[...], g).astype(BF16)
        keep = jnp.where((i * tm) % seq == 0, 0.0, 1.0)
        u_ref[:CONV_HALO, :] = (_rms_rows(halo_ref[...], g) * keep).astype(BF16)

    u = u_ref[...]
    ag = _causal_conv3(jnp.dot(u, wg_ref[...], preferred_element_type=F32), cwg_ref[...], cbg_ref[...])
    au = _causal_conv3(jnp.dot(u, wu_ref[...], preferred_element_type=F32), cwu_ref[...], cbu_ref[...])
    o_ref[...] = (_gelu_tanh(ag) * au).astype(o_ref.dtype)


def ffn_up(x, g, w_up, conv_w, conv_b, seq, *, tm=1024, tn=512):
    m, d = x.shape
    f = w_up.shape[1] // 2
    nn = f // tn
    halo_blocks = tm // CONV_HALO
    cb = conv_b.reshape(1, 2 * f)
    return pl.pallas_call(
        functools.partial(_ffn_up_kernel, tm=tm, seq=seq),
        grid=(m // tm, nn),
        in_specs=[
            pl.BlockSpec((tm, d), lambda i, j: (i, 0)),
            pl.BlockSpec((CONV_HALO, d), lambda i, j: (jnp.maximum(i * halo_blocks - 1, 0), 0)),
            pl.BlockSpec((1, d), lambda i, j: (0, 0)),
            pl.BlockSpec((d, tn), lambda i, j: (0, j)),
            pl.BlockSpec((d, tn), lambda i, j: (0, nn + j)),
            pl.BlockSpec((CONV_WIDTH, tn), lambda i, j: (0, j)),
            pl.BlockSpec((CONV_WIDTH, tn), lambda i, j: (0, nn + j)),
            pl.BlockSpec((1, tn), lambda i, j: (0, j)),
            pl.BlockSpec((1, tn), lambda i, j: (0, nn + j)),
        ],
        out_specs=pl.BlockSpec((tm, tn), lambda i, j: (i, j)),
        out_shape=jax.ShapeDtypeStruct((m, f), BF16),
        scratch_shapes=[pltpu.VMEM((tm + CONV_HALO, d), BF16)],
        compiler_params=_params(("parallel", "arbitrary")),
        name="ffn_up_conv_glu",
    )(x, x, g.reshape(1, d), w_up, w_up, conv_w, conv_w, cb, cb)


def _rel_bucket_np(dist):
    dist = np.maximum(dist, 0)
    max_exact = N_BUCKETS // 2
    d = np.maximum(dist, 1).astype(np.float32)
    large = max_exact + (np.log(d / max_exact) / math.log(MAX_DISTANCE / max_exact)
                         * (N_BUCKETS - max_exact)).astype(np.int32)
    large = np.minimum(large, N_BUCKETS - 1)
    return np.where(dist < max_exact, dist, large)


def _strip_distances(n_delta, t):
    return np.arange(n_delta)[:, None] * t + np.arange(2 * t)[None, :] - t


def _causal_log_mult(dist):
    return np.where(dist >= 0, 0.0, -np.inf)


def _dilated_log_mult(dist):
    mult = np.zeros(dist.shape, np.float64)
    for window, dil in DILATED_GROUPS:
        mult += (dist >= 0) & (dist <= window) & (dist % dil == 0)
    with np.errstate(divide="ignore"):
        return np.log(mult)


def _bias_strips(rel_bias, dist, log_mult):
    shown = np.isfinite(log_mult)
    offset = jnp.asarray(np.where(shown, log_mult * LOG2E, NEG), F32)
    if rel_bias is None:
        return offset[None]
    bias = jnp.take(rel_bias.astype(F32), jnp.asarray(_rel_bucket_np(dist)), axis=0)
    return jnp.moveaxis(jnp.where(jnp.asarray(shown)[..., None], bias * LOG2E + offset[..., None], NEG), -1, 0)


def _attn_kernel(*refs, kind, g, t, seq, n_delta, lambda_init):
    n_in = {"dilated": 4, "moba": 4, "diff": 6, "fox": 6}[kind]
    q_ref, k_ref, v_ref, strip_ref = refs[:4]
    o_ref = refs[n_in]
    tab_ref, vt_ref, s_ref, p_ref = refs[n_in + 1:n_in + 5]
    hg = pl.program_id(0)
    nk = seq // t
    n_tab = tab_ref.shape[0]
    n_maps = s_ref.shape[0]
    dh = HEAD_DIM

    @pl.when(pl.program_id(1) == 0)
    def _():
        for hh in range(n_tab):
            for dl in range(n_delta):
                strip = jnp.broadcast_to(strip_ref[hh, dl:dl + 1, :], (t, 2 * t))
                tab_ref[hh, dl] = pltpu.roll(strip, 0, 1, stride=1, stride_axis=0)[:, t:]

    def tile(j):
        return slice(j * t, (j + 1) * t)

    for hh in range(g):
        head = slice(hh * dh, (hh + 1) * dh)
        for jb in range(nk):
            vt_ref[:, tile(jb)] = v_ref[tile(jb), head].astype(F32).T.astype(BF16)
        if kind == "moba":
            kmean_ref = refs[n_in + 5]
            for jb in range(nk):
                kmean_ref[jb:jb + 1, :] = jnp.mean(k_ref[tile(jb), head].astype(F32), axis=0, keepdims=True)
            km = kmean_ref[...]
            km_hi = km.astype(BF16)
            km_lo = (km - km_hi.astype(F32)).astype(BF16)
        if kind == "fox":
            cs_ref, cst_ref, ckb_ref = refs[4], refs[5], refs[n_in + 5]
            cs = cs_ref[...]
            lane = lax.broadcasted_iota(jnp.int32, cs.shape, 1)
            ck = jnp.sum(jnp.where(lane == hg * g + hh, cs, 0.0), axis=-1, keepdims=True)
            ckb_ref[...] = jnp.broadcast_to(ck, cs.shape)
        if kind == "diff":
            lam_ref, sub_ref = refs[4], refs[5]
            lam = lam_ref[...]
            lmbda = (jnp.exp(jnp.sum(lam[0:1, :] * lam[1:2, :], axis=-1, keepdims=True))
                     - jnp.exp(jnp.sum(lam[2:3, :] * lam[3:4, :], axis=-1, keepdims=True)) + lambda_init)

        for i in range(nk):
            keys = slice(0, (i + 1) * t)
            qt = q_ref[tile(i), head].astype(F32).T.astype(BF16)
            if kind == "diff":
                row = lax.broadcasted_iota(jnp.int32, qt.shape, 0)
                qts = [jnp.where(row < DIFF_HEAD_DIM, qt, jnp.zeros_like(qt)),
                       jnp.where(row >= DIFF_HEAD_DIM, qt, jnp.zeros_like(qt))]
            else:
                qts = [qt]
            sel = None
            if kind == "moba" and i > 0:
                gate = (jnp.dot(km_hi, qt, preferred_element_type=F32)
                        + jnp.dot(km_lo, qt, preferred_element_type=F32))
                blk = lax.broadcasted_iota(jnp.int32, gate.shape, 0)
                sel = []
                for jb in range(i):
                    gj = gate[jb:jb + 1, :]
                    beats = (blk < i) & ((gate > gj) | ((gate == gj) & (blk < jb)))
                    rank = jnp.sum(jnp.where(beats, 1.0, 0.0), axis=0, keepdims=True)
                    sel.append(jnp.where(rank < MOBA_TOPK, 0.0, NEG))

            outs = []
            for mp in range(n_maps):
                m = None
                for j in range(i + 1):
                    s = jnp.dot(k_ref[tile(j), head], qts[mp], preferred_element_type=F32)
                    if kind == "fox":
                        s = s + cst_ref[hh, i:i + 1, :] - jnp.concatenate([ckb_ref[tile(j), :]] * (t // dh), axis=1)
                    s = s + tab_ref[hh % n_tab, min(i - j, n_delta - 1)]
                    if kind == "moba" and j < i:
                        s = s + sel[j]
                    s_ref[mp, tile(j), :] = s
                    tile_max = jnp.max(s, axis=0, keepdims=True)
                    m = tile_max if m is None else jnp.maximum(m, tile_max)
                den = None
                for j in range(i + 1):
                    p = jnp.exp2(s_ref[mp, tile(j), :] - m)
                    p_ref[mp, tile(j), :] = p.astype(BF16)
                    tile_sum = jnp.sum(p, axis=0, keepdims=True)
                    den = tile_sum if den is None else den + tile_sum
                acc = jnp.dot(vt_ref[:, keys], p_ref[mp, keys, :], preferred_element_type=F32)
                outs.append(acc / den)
            if kind == "diff":
                out = _rms_rows((outs[0] - lmbda * outs[1]).T, sub_ref[...]) * (1.0 - lambda_init)
            else:
                out = outs[0].T
            o_ref[tile(i), head] = out.astype(o_ref.dtype)


def attention(kind, qkv, strips, batch, seq, extra=(), lambda_init=0.0):
    t, g, dh = ATTN_TILE, ATTN_HEADS_PER_STEP, HEAD_DIM
    ng = N_HEADS // g
    m = qkv.shape[0]
    per_head = strips.shape[0] == N_HEADS
    n_tab = g if per_head else 1
    n_delta = strips.shape[1]
    n_maps = 2 if kind == "diff" else 1
    in_specs = [
        pl.BlockSpec((seq, g * dh), lambda h, b: (b, h)),
        pl.BlockSpec((seq, g * dh), lambda h, b: (b, ng + h)),
        pl.BlockSpec((seq, g * dh), lambda h, b: (b, 2 * ng + h)),
        pl.BlockSpec((n_tab, n_delta, 2 * t), (lambda h, b: (h, 0, 0)) if per_head else (lambda h, b: (0, 0, 0))),
    ]
    args = [qkv, qkv, qkv, strips]
    scratch = [
        pltpu.VMEM((n_tab, n_delta, t, t), F32),
        pltpu.VMEM((dh, seq), BF16),
        pltpu.VMEM((n_maps, seq, t), F32),
        pltpu.VMEM((n_maps, seq, t), BF16),
    ]
    if kind == "moba":
        scratch.append(pltpu.VMEM((seq // t, dh), F32))
    elif kind == "diff":
        lam, subln = extra
        in_specs += [pl.BlockSpec(lam.shape, lambda h, b: (0, 0)),
                     pl.BlockSpec((1, dh), lambda h, b: (0, 0))]
        args += [lam, subln.reshape(1, dh)]
    elif kind == "fox":
        csum, csum_t = extra
        in_specs += [pl.BlockSpec((None, seq, dh), lambda h, b: (b, 0, 0)),
                     pl.BlockSpec((None, g, seq // t, t), lambda h, b: (b, h, 0, 0))]
        args += [csum, csum_t]
        scratch.append(pltpu.VMEM((seq, dh), F32))
    return pl.pallas_call(
        functools.partial(_attn_kernel, kind=kind, g=g, t=t, seq=seq, n_delta=n_delta, lambda_init=lambda_init),
        grid=(ng, batch),
        in_specs=in_specs,
        out_specs=pl.BlockSpec((seq, g * dh), lambda h, b: (b, h)),
        out_shape=jax.ShapeDtypeStruct((m, N_HEADS * dh), BF16),
        scratch_shapes=scratch,
        compiler_params=_params(("parallel", "arbitrary")),
        name=kind + "_attention",
    )(*args)


def _fox_gate_kernel(f_ref, b_ref, cs_ref, cst_ref):
    z = f_ref[...] + b_ref[...]
    x = jnp.minimum(z, 0.0) - jnp.log1p(jnp.exp(-jnp.abs(z)))
    row = lax.broadcasted_iota(jnp.int32, x.shape, 0)
    shift = 1
    while shift < x.shape[0]:
        x = x + jnp.where(row >= shift, pltpu.roll(x, shift, 0), 0.0)
        shift *= 2
    x = x * LOG2E
    cs_ref[...] = x
    cst_ref[...] = x.T


def fox_gate(f, b_pad):
    batch, seq, lanes = f.shape
    return pl.pallas_call(
        _fox_gate_kernel,
        grid=(batch,),
        in_specs=[pl.BlockSpec((None, seq, lanes), lambda b: (b, 0, 0)),
                  pl.BlockSpec((1, lanes), lambda b: (0, 0))],
        out_specs=[pl.BlockSpec((None, seq, lanes), lambda b: (b, 0, 0)),
                   pl.BlockSpec((None, lanes, seq), lambda b: (b, 0, 0))],
        out_shape=[jax.ShapeDtypeStruct((batch, seq, lanes), F32),
                   jax.ShapeDtypeStruct((batch, lanes, seq), F32)],
        compiler_params=_params(("parallel",)),
        name="fox_gate_cumsum",
    )(f, b_pad)


def kernel(x, rel_bias, norm_gains, w_out, w_up, conv_w, conv_b, w_down, w_in_dil, w_in_moba,
           w_in_diff, lambda_diff, subln_diff, w_in_fox, b_fox):
    batch, seq, d = x.shape
    depth = norm_gains.shape[0]
    t = ATTN_TILE
    assert seq % t == 0 and t == MOBA_BLOCK and d == N_HEADS * HEAD_DIM
    causal3 = _strip_distances(3, t)
    causal_strips = _bias_strips(rel_bias, causal3, _causal_log_mult(causal3))

    def qkv_scale(head_scale):
        return jnp.concatenate([jnp.full((d,), head_scale * LOG2E, F32), jnp.ones((2 * d,), F32)])

    h = x.reshape(batch * seq, d)
    for i in range(depth):
        mixer, j = i % N_MIXERS, i // N_MIXERS
        g = norm_gains[i]
        if mixer == 0:
            qkv = norm_matmul(h, g[0], w_in_dil[j].astype(BF16), qkv_scale(HEAD_DIM ** -0.5))
            dist = _strip_distances(4, t)
            y = attention("dilated", qkv, _bias_strips(rel_bias, dist, _dilated_log_mult(dist)), batch, seq)
        elif mixer == 1:
            qkv = norm_matmul(h, g[0], w_in_moba[j].astype(BF16), qkv_scale(HEAD_DIM ** -0.5))
            y = attention("moba", qkv, causal_strips, batch, seq)
        elif mixer == 2:
            qkv = norm_matmul(h, g[0], w_in_diff[j].astype(BF16), qkv_scale(DIFF_HEAD_DIM ** -0.5))
            y = attention("diff", qkv, causal_strips, batch, seq, extra=(lambda_diff[j], subln_diff[j]),
                          lambda_init=0.8 - 0.6 * math.exp(-0.3 * i))
        else:
            w = w_in_fox[j]
            w_f = jnp.pad(w[:, 3 * d:], ((0, 0), (0, HEAD_DIM - N_HEADS))).astype(BF16)
            qkv, f = norm_matmul(h, g[0], w[:, :3 * d].astype(BF16), qkv_scale(HEAD_DIM ** -0.5), w_f)
            b_pad = jnp.pad(b_fox[j], (0, HEAD_DIM - N_HEADS)).reshape(1, HEAD_DIM)
            csum, csum_t = fox_gate(f.reshape(batch, seq, HEAD_DIM), b_pad)
            dist = _strip_distances(2, t)
            y = attention("fox", qkv, _bias_strips(None, dist, _causal_log_mult(dist)), batch, seq,
                          extra=(csum, csum_t.reshape(batch, HEAD_DIM, seq // t, t)))
        h = matmul_norm_res(y, w_out[i].astype(BF16), g[1], h, tk=d)
        hidden = ffn_up(h, g[2], w_up[i].astype(BF16), conv_w[i], conv_b[i], seq)
        h = matmul_norm_res(hidden, w_down[i].astype(BF16), g[3], h)
    return h.reshape(batch, seq, d)
```

```python
import functools
import math

import jax
import jax.numpy as jnp
import numpy as np
from jax import lax
from jax.experimental import pallas as pl
from jax.experimental.pallas import tpu as pltpu

N_HEADS = 16
HEAD_DIM = 128
N_MIXERS = 4
CONV_WIDTH = 3
RMS_EPS = 1e-6
N_BUCKETS = 32
MAX_DISTANCE = 128
DILATED_GROUPS = ((128, 1), (512, 4), (2048, 16))
MOBA_BLOCK = 256
MOBA_TOPK = 3
DIFF_HEAD_DIM = HEAD_DIM // 2
NEG = -1e30
LOG2E = math.log2(math.e)

ATTN_TILE = 256
ATTN_HEADS_PER_STEP = 1
CONV_HALO = 16
CAST_BLOCK_ELEMS =2 * 1024 * 1024
VMEM_LIMIT = 52 * 1024 * 1024

BF16 = jnp.bfloat16
F32 = jnp.float32


def _rms_rows(x, g):
    return x * lax.rsqrt(jnp.mean(x * x, axis=-1, keepdims=True) + RMS_EPS) * g


def _params(sem, vmem=VMEM_LIMIT):
    return pltpu.CompilerParams(dimension_semantics=sem, vmem_limit_bytes=vmem)


def _cast_kernel(x_ref, o_ref):
    o_ref[...] = x_ref[...].astype(o_ref.dtype)


def cast_bf16(w):
    layers, k, n = w.shape
    rows = 1 << (min(k, CAST_BLOCK_ELEMS // n).bit_length() - 1)
    assert k % rows == 0
    spec = pl.BlockSpec((None, rows, n), lambda l, r: (l, r, 0))
    return pl.pallas_call(
        _cast_kernel,
        grid=(layers, k // rows),
        in_specs=[spec],
        out_specs=spec,
        out_shape=jax.ShapeDtypeStruct(w.shape, BF16),
        compiler_params=_params(("parallel", "parallel")),
        name="cast_bf16",
    )(w)


def _norm_matmul_kernel(x_ref, g_ref, w_ref, cs_ref, *rest, has_extra):
    if has_extra:
        wx_ref, o_ref, ox_ref, u_ref = rest
    else:
        o_ref, u_ref = rest

    @pl.when(pl.program_id(1) == 0)
    def _():
        u_ref[...] = _rms_rows(x_ref[...], g_ref[...]).astype(BF16)
        if has_extra:
            ox_ref[...] = jnp.dot(u_ref[...], wx_ref[...], preferred_element_type=F32)

    acc = jnp.dot(u_ref[...], w_ref[...], preferred_element_type=F32)
    o_ref[...] = (acc * cs_ref[...]).astype(o_ref.dtype)


def norm_matmul(x, g, w, layer, col_scale, w_extra=None, *, tm=1024, tn=1024):
    m, d = x.shape
    n = col_scale.shape[0]
    has_extra = w_extra is not None
    in_specs = [
        pl.BlockSpec((tm, d), lambda i, j: (i, 0)),
        pl.BlockSpec((1, d), lambda i, j: (0, 0)),
        pl.BlockSpec((None, d, tn), lambda i, j: (layer, 0, j)),
        pl.BlockSpec((1, tn), lambda i, j: (0, j)),
    ]
    out_shape = [jax.ShapeDtypeStruct((m, n), BF16)]
    out_specs = [pl.BlockSpec((tm, tn), lambda i, j: (i, j))]
    args = [x, g.reshape(1, d), w, col_scale.reshape(1, n)]
    if has_extra:
        nx = w_extra.shape[1]
        in_specs.append(pl.BlockSpec((d, nx), lambda i, j: (0, 0)))
        out_shape.append(jax.ShapeDtypeStruct((m, nx), F32))
        out_specs.append(pl.BlockSpec((tm, nx), lambda i, j: (i, 0)))
        args.append(w_extra)
    outs = pl.pallas_call(
        functools.partial(_norm_matmul_kernel, has_extra=has_extra),
        grid=(m // tm, n // tn),
        in_specs=in_specs,
        out_specs=out_specs,
        out_shape=out_shape,
        scratch_shapes=[pltpu.VMEM((tm, d), BF16)],
        compiler_params=_params(("parallel", "arbitrary")),
        name="norm_matmul_extra" if has_extra else "norm_matmul",
    )(*args)
    return outs if has_extra else outs[0]


def _matmul_norm_res_kernel(y_ref, w_ref, g_ref, res_ref, o_ref, *, nk):
    k = pl.program_id(1)
    if nk == 1:
        o_ref[...] = res_ref[...] + _rms_rows(jnp.dot(y_ref[...], w_ref[...], preferred_element_type=F32),
                                              g_ref[...])
        return

    @pl.when(k == 0)
    def _():
        o_ref[...] = jnp.dot(y_ref[...], w_ref[...], preferred_element_type=F32)

    @pl.when(k > 0)
    def _():
        o_ref[...] += jnp.dot(y_ref[...], w_ref[...], preferred_element_type=F32)

    @pl.when(k == nk - 1)
    def _():
        o_ref[...] = res_ref[...] + _rms_rows(o_ref[...], g_ref[...])


def matmul_norm_res(y, w, layer, g, res, *, tm=512, tk=2048):
    m, kdim = y.shape
    d = w.shape[2]
    nk = kdim // tk
    return pl.pallas_call(
        functools.partial(_matmul_norm_res_kernel, nk=nk),
        grid=(m // tm, nk),
        in_specs=[
            pl.BlockSpec((tm, tk), lambda i, k: (i, k)),
            pl.BlockSpec((None, tk, d), lambda i, k: (layer, k, 0)),
            pl.BlockSpec((1, d), lambda i, k: (0, 0)),
            pl.BlockSpec((tm, d), lambda i, k: (i, 0)),
        ],
        out_specs=pl.BlockSpec((tm, d), lambda i, k: (i, 0)),
        out_shape=jax.ShapeDtypeStruct((m, d), F32),
        compiler_params=_params(("parallel", "arbitrary")),
        name="matmul_norm_res",
    )(y, w, g.reshape(1, d), res)


def _gelu_tanh(x):
    return x * (0.5 * (1.0 + jnp.tanh(math.sqrt(2.0 / math.pi) * (x + 0.044715 * (x * x * x)))))


def _causal_conv3(a, cw, cb):
    r1 = pltpu.roll(a, 1, 0)
    r2 = pltpu.roll(a, 2, 0)
    y = cw[0:1, :] * r2 + cw[1:2, :] * r1 + cw[2:3, :] * a
    return y[CONV_HALO:, :] + cb


def _ffn_up_kernel(x_ref, halo_ref, g_ref, wg_ref, wu_ref, cwg_ref, cwu_ref, cbg_ref, cbu_ref,
                   o_ref, u_ref, *, tm, seq):
    i = pl.program_id(0)

    @pl.when(pl.program_id(1) == 0)
    def _():
        g = g_ref[...]
        u_ref[CONV_HALO:, :] = _rms_rows(x_ref[...], g).astype(BF16)
        keep = jnp.where((i * tm) % seq == 0, 0.0, 1.0)
        u_ref[:CONV_HALO, :] = (_rms_rows(halo_ref[...], g) * keep).astype(BF16)

    u = u_ref[...]
    ag = _causal_conv3(jnp.dot(u, wg_ref[...], preferred_element_type=F32), cwg_ref[...], cbg_ref[...])
    au = _causal_conv3(jnp.dot(u, wu_ref[...], preferred_element_type=F32), cwu_ref[...], cbu_ref[...])
    o_ref[...] = (_gelu_tanh(ag) * au).astype(o_ref.dtype)


def ffn_up(x, g, w_up, layer, conv_w, conv_b, seq, *, tm=1024, tn=512):
    m, d = x.shape
    f = w_up.shape[2] // 2
    nn = f // tn
    halo_blocks = tm // CONV_HALO
    cb = conv_b.reshape(1, 2 * f)
    return pl.pallas_call(
        functools.partial(_ffn_up_kernel, tm=tm, seq=seq),
        grid=(m // tm, nn),
        in_specs=[
            pl.BlockSpec((tm, d), lambda i, j: (i, 0)),
            pl.BlockSpec((CONV_HALO, d), lambda i, j: (jnp.maximum(i * halo_blocks - 1, 0), 0)),
            pl.BlockSpec((1, d), lambda i, j: (0, 0)),
            pl.BlockSpec((None, d, tn), lambda i, j: (layer, 0, j)),
            pl.BlockSpec((None, d, tn), lambda i, j: (layer, 0, nn + j)),
            pl.BlockSpec((CONV_WIDTH, tn), lambda i, j: (0, j)),
            pl.BlockSpec((CONV_WIDTH, tn), lambda i, j: (0, nn + j)),
            pl.BlockSpec((1, tn), lambda i, j: (0, j)),
            pl.BlockSpec((1, tn), lambda i, j: (0, nn + j)),
        ],
        out_specs=pl.BlockSpec((tm, tn), lambda i, j: (i, j)),
        out_shape=jax.ShapeDtypeStruct((m, f), BF16),
        scratch_shapes=[pltpu.VMEM((tm + CONV_HALO, d), BF16)],
        compiler_params=_params(("parallel", "arbitrary")),
        name="ffn_up_conv_glu",
    )(x, x, g.reshape(1, d), w_up, w_up, conv_w, conv_w, cb, cb)


def _rel_bucket_np(dist):
    dist = np.maximum(dist, 0)
    max_exact = N_BUCKETS // 2
    d = np.maximum(dist, 1).astype(np.float32)
    large = max_exact + (np.log(d / max_exact) / math.log(MAX_DISTANCE / max_exact)
                         * (N_BUCKETS - max_exact)).astype(np.int32)
    large = np.minimum(large, N_BUCKETS - 1)
    return np.where(dist < max_exact, dist, large)


def _strip_distances(n_delta, t):
    return np.arange(n_delta)[:, None] * t + np.arange(2 * t)[None, :] - t


def _causal_log_mult(dist):
    return np.where(dist >= 0, 0.0, -np.inf)


def _dilated_log_mult(dist):
    mult = np.zeros(dist.shape, np.float64)
    for window, dil in DILATED_GROUPS:
        mult += (dist >= 0) & (dist <= window) & (dist % dil == 0)
    with np.errstate(divide="ignore"):
        return np.log(mult)


def _bias_strips(rel_bias, dist, log_mult):
    shown = np.isfinite(log_mult)
    offset = jnp.asarray(np.where(shown, log_mult * LOG2E, NEG), F32)
    if rel_bias is None:
        return offset[None]
    bias = jnp.take(rel_bias.astype(F32), jnp.asarray(_rel_bucket_np(dist)), axis=0)
    return jnp.moveaxis(jnp.where(jnp.asarray(shown)[..., None], bias * LOG2E + offset[..., None], NEG), -1, 0)


def _attn_kernel(*refs, kind, g, t, seq, n_delta, lambda_init):
    n_in = {"dilated": 4, "moba": 4, "diff": 6, "fox": 6}[kind]
    q_ref, k_ref, v_ref, strip_ref = refs[:4]
    o_ref = refs[n_in]
    tab_ref, vt_ref, s_ref, p_ref = refs[n_in + 1:n_in + 5]
    hg = pl.program_id(0)
    nk = seq // t
    n_tab = tab_ref.shape[0]
    n_maps = s_ref.shape[0] // 2
    dh = HEAD_DIM

    @pl.when(pl.program_id(1) == 0)
    def _():
        for hh in range(n_tab):
            for dl in range(n_delta):
                strip = jnp.broadcast_to(strip_ref[hh, dl:dl + 1, :], (t, 2 * t))
                tab_ref[hh, dl] = pltpu.roll(strip, 0, 1, stride=1, stride_axis=0)[:, t:]

    def tile(j):
        return slice(j * t, (j + 1) * t)

    for hh in range(g):
        head = slice(hh * dh, (hh + 1) * dh)
        for jb in range(nk):
            vt_ref[:, tile(jb)] = v_ref[tile(jb), head].T
        if kind == "moba":
            kmean_ref = refs[n_in + 5]
            for jb in range(nk):
                kmean_ref[jb:jb + 1, :] = jnp.mean(k_ref[tile(jb), head].astype(F32), axis=0, keepdims=True)
            km = kmean_ref[...]
            km_hi = km.astype(BF16)
            km_lo = (km - km_hi.astype(F32)).astype(BF16)
        if kind == "fox":
            cs_ref, cst_ref, ckb_ref = refs[4], refs[5], refs[n_in + 5]
            cs = cs_ref[...]
            lane = lax.broadcasted_iota(jnp.int32, cs.shape, 1)
            ck = jnp.sum(jnp.where(lane == hg * g + hh, cs, 0.0), axis=-1, keepdims=True)
            ckb_ref[...] = jnp.broadcast_to(ck, cs.shape)
        if kind == "diff":
            lam_ref, sub_ref = refs[4], refs[5]
            lam = lam_ref[...]
            lmbda = (jnp.exp(jnp.sum(lam[0:1, :] * lam[1:2, :], axis=-1, keepdims=True))
                     - jnp.exp(jnp.sum(lam[2:3, :] * lam[3:4, :], axis=-1, keepdims=True)) + lambda_init)

        def score_pass(i):
            qt = q_ref[tile(i), head].T
            if kind == "diff":
                row = lax.broadcasted_iota(jnp.int32, qt.shape, 0)
                qts = [jnp.where(row < DIFF_HEAD_DIM, qt, jnp.zeros_like(qt)),
                       jnp.where(row >= DIFF_HEAD_DIM, qt, jnp.zeros_like(qt))]
            else:
                qts = [qt]
            sel = None
            if kind == "moba" and i > 0:
                gate = (jnp.dot(km_hi, qt, preferred_element_type=F32)
                        + jnp.dot(km_lo, qt, preferred_element_type=F32))
                blk = lax.broadcasted_iota(jnp.int32, gate.shape, 0)
                sel = []
                for jb in range(i):
                    gj = gate[jb:jb + 1, :]
                    beats = (blk < i) & ((gate > gj) | ((gate == gj) & (blk < jb)))
                    rank = jnp.sum(jnp.where(beats, 1.0, 0.0), axis=0, keepdims=True)
                    sel.append(jnp.where(rank < MOBA_TOPK, 0.0, NEG))
            maxima = []
            for mp in range(n_maps):
                buf = (i % 2) * n_maps + mp
                m = None
                for j in range(i + 1):
                    s = jnp.dot(k_ref[tile(j), head], qts[mp], preferred_element_type=F32)
                    if kind == "fox":
                        s = s + cst_ref[hh, i:i + 1, :] - jnp.concatenate([ckb_ref[tile(j), :]] * (t // dh), axis=1)
                    s = s + tab_ref[hh % n_tab, min(i - j, n_delta - 1)]
                    if kind == "moba" and j < i:
                        s = s + sel[j]
                    s_ref[buf, tile(j), :] = s
                    tile_max = jnp.max(s, axis=0, keepdims=True)
                    m = tile_max if m is None else jnp.maximum(m, tile_max)
                maxima.append(m)
            return maxima

        def value_pass(i, maxima):
            keys = slice(0, (i + 1) * t)
            outs = []
            for mp in range(n_maps):
                buf = (i % 2) * n_maps + mp
                den = None
                for j in range(i + 1):
                    p = jnp.exp2(s_ref[buf, tile(j), :] - maxima[mp])
                    p_ref[buf, tile(j), :] = p.astype(BF16)
                    tile_sum = jnp.sum(p, axis=0, keepdims=True)
                    den = tile_sum if den is None else den + tile_sum
                acc = jnp.dot(vt_ref[:, keys], p_ref[buf, keys, :], preferred_element_type=F32)
                outs.append(acc / den)
            if kind == "diff":
                out = _rms_rows((outs[0] - lmbda * outs[1]).T, sub_ref[...]) * (1.0 - lambda_init)
            else:
                out = outs[0].T
            o_ref[tile(i), head] = out.astype(o_ref.dtype)

        maxima = score_pass(0)
        for i in range(nk):
            next_maxima = score_pass(i + 1) if i + 1 < nk else None
            value_pass(i, maxima)
            maxima = next_maxima


def attention(kind, qkv, strips, batch, seq, extra=(), lambda_init=0.0):
    t, g, dh = ATTN_TILE, ATTN_HEADS_PER_STEP, HEAD_DIM
    ng = N_HEADS // g
    m = qkv.shape[0]
    per_head = strips.shape[0] == N_HEADS
    n_tab = g if per_head else 1
    n_delta = strips.shape[1]
    n_maps = 2 if kind == "diff" else 1
    in_specs = [
        pl.BlockSpec((seq, g * dh), lambda h, b: (b, h)),
        pl.BlockSpec((seq, g * dh), lambda h, b: (b, ng + h)),
        pl.BlockSpec((seq, g * dh), lambda h, b: (b, 2 * ng + h)),
        pl.BlockSpec((n_tab, n_delta, 2 * t), (lambda h, b: (h, 0, 0)) if per_head else (lambda h, b: (0, 0, 0))),
    ]
    args = [qkv, qkv, qkv, strips]
    scratch = [
        pltpu.VMEM((n_tab, n_delta, t, t), F32),
        pltpu.VMEM((dh, seq), BF16),
        pltpu.VMEM((2 * n_maps, seq, t), F32),
        pltpu.VMEM((2 * n_maps, seq, t), BF16),
    ]
    if kind == "moba":
        scratch.append(pltpu.VMEM((seq // t, dh), F32))
    elif kind == "diff":
        lam, subln = extra
        in_specs += [pl.BlockSpec(lam.shape, lambda h, b: (0, 0)),
                     pl.BlockSpec((1, dh), lambda h, b: (0, 0))]
        args += [lam, subln.reshape(1, dh)]
    elif kind == "fox":
        csum, csum_t = extra
        in_specs += [pl.BlockSpec((None, seq, dh), lambda h, b: (b, 0, 0)),
                     pl.BlockSpec((None, g, seq // t, t), lambda h, b: (b, h, 0, 0))]
        args += [csum, csum_t]
        scratch.append(pltpu.VMEM((seq, dh), F32))
    return pl.pallas_call(
        functools.partial(_attn_kernel, kind=kind, g=g, t=t, seq=seq, n_delta=n_delta, lambda_init=lambda_init),
        grid=(ng, batch),
        in_specs=in_specs,
        out_specs=pl.BlockSpec((seq, g * dh), lambda h, b: (b, h)),
        out_shape=jax.ShapeDtypeStruct((m, N_HEADS * dh), BF16),
        scratch_shapes=scratch,
        compiler_params=_params(("parallel", "arbitrary")),
        name=kind + "_attention",
    )(*args)


def _fox_gate_kernel(f_ref, b_ref, cs_ref, cst_ref):
    z = f_ref[...] + b_ref[...]
    x = jnp.minimum(z, 0.0) - jnp.log1p(jnp.exp(-jnp.abs(z)))
    row = lax.broadcasted_iota(jnp.int32, x.shape, 0)
    shift = 1
    while shift < x.shape[0]:
        x = x + jnp.where(row >= shift, pltpu.roll(x, shift, 0), 0.0)
        shift *= 2
    x = x * LOG2E
    cs_ref[...] = x
    cst_ref[...] = x.T


def fox_gate(f, b_pad):
    batch, seq, lanes = f.shape
    return pl.pallas_call(
        _fox_gate_kernel,
        grid=(batch,),
        in_specs=[pl.BlockSpec((None, seq, lanes), lambda b: (b, 0, 0)),
                  pl.BlockSpec((1, lanes), lambda b: (0, 0))],
        out_specs=[pl.BlockSpec((None, seq, lanes), lambda b: (b, 0, 0)),
                   pl.BlockSpec((None, lanes, seq), lambda b: (b, 0, 0))],
        out_shape=[jax.ShapeDtypeStruct((batch, seq, lanes), F32),
                   jax.ShapeDtypeStruct((batch, lanes, seq), F32)],
        compiler_params=_params(("parallel",)),
        name="fox_gate_cumsum",
    )(f, b_pad)


def kernel(x, rel_bias, norm_gains, w_out, w_up, conv_w, conv_b, w_down, w_in_dil, w_in_moba,
           w_in_diff, lambda_diff, subln_diff, w_in_fox, b_fox):
    batch, seq, d = x.shape
    depth = norm_gains.shape[0]
    t = ATTN_TILE
    assert seq % t == 0 and t == MOBA_BLOCK and d == N_HEADS * HEAD_DIM
    causal3 = _strip_distances(3, t)
    causal_strips = _bias_strips(rel_bias, causal3, _causal_log_mult(causal3))

    def qkv_scale(head_scale):
        return jnp.concatenate([jnp.full((d,), head_scale * LOG2E, F32), jnp.ones((2 * d,), F32)])

    w_out_b, w_up_b, w_down_b = cast_bf16(w_out), cast_bf16(w_up), cast_bf16(w_down)
    w_dil_b, w_moba_b, w_diff_b, w_fox_b = (cast_bf16(w) for w in (w_in_dil, w_in_moba, w_in_diff, w_in_fox))

    h = x.reshape(batch * seq, d)
    for i in range(depth):
        mixer, j = i % N_MIXERS, i // N_MIXERS
        g = norm_gains[i]
        if mixer == 0:
            qkv = norm_matmul(h, g[0], w_dil_b, j, qkv_scale(HEAD_DIM ** -0.5))
            dist = _strip_distances(4, t)
            y = attention("dilated", qkv, _bias_strips(rel_bias, dist, _dilated_log_mult(dist)), batch, seq)
        elif mixer == 1:
            qkv = norm_matmul(h, g[0], w_moba_b, j, qkv_scale(HEAD_DIM ** -0.5))
            y = attention("moba", qkv, causal_strips, batch, seq)
        elif mixer == 2:
            qkv = norm_matmul(h, g[0], w_diff_b, j, qkv_scale(DIFF_HEAD_DIM ** -0.5))
            y = attention("diff", qkv, causal_strips, batch, seq, extra=(lambda_diff[j], subln_diff[j]),
                          lambda_init=0.8 - 0.6 * math.exp(-0.3 * i))
        else:
            w_f = jnp.pad(w_fox_b[j, :, 3 * d:], ((0, 0), (0, HEAD_DIM - N_HEADS)))
            qkv, f = norm_matmul(h, g[0], w_fox_b, j, qkv_scale(HEAD_DIM ** -0.5), w_f)
            b_pad = jnp.pad(b_fox[j], (0, HEAD_DIM - N_HEADS)).reshape(1, HEAD_DIM)
            csum, csum_t = fox_gate(f.reshape(batch, seq, HEAD_DIM), b_pad)
            dist = _strip_distances(2, t)
            y = attention("fox", qkv, _bias_strips(None, dist, _causal_log_mult(dist)), batch, seq,
                          extra=(csum, csum_t.reshape(batch, HEAD_DIM, seq // t, t)))
        h = matmul_norm_res(y, w_out_b, i, g[1], h, tk=d)
        hidden = ffn_up(h, g[2], w_up_b, i, conv_w[i], conv_b[i], seq)
        h = matmul_norm_res(hidden, w_down_b, i, g[3], h)
    return h.reshape(batch, seq, d)
```

```python
import functools
import math

import jax
import jax.numpy as jnp
import numpy as np
from jax import lax
from jax.experimental import pallas as pl
from jax.experimental.pallas import tpu as pltpu

N_HEADS = 16
HEAD_DIM = 128
N_MIXERS = 4
CONV_WIDTH = 3
RMS_EPS = 1e-6
N_BUCKETS = 32
MAX_DISTANCE = 128
DILATED_GROUPS = ((128, 1), (512, 4), (2048, 16))
MOBA_BLOCK = 256
MOBA_TOPK = 3
DIFF_HEAD_DIM = HEAD_DIM // 2
NEG = -1e30
LOG2E = math.log2(math.e)

ATTN_TILE = 256
ATTN_HEADS_PER_STEP = 1
DEN_ROWS = 16
CONV_HALO = 16
CAST_BLOCK_ELEMS = 2 * 1024 * 1024
VMEM_LIMIT = 52 * 1024 * 1024

BF16 = jnp.bfloat16
F32 = jnp.float32


def _rms_rows(x, g):
    return x * lax.rsqrt(jnp.mean(x * x, axis=-1, keepdims=True) + RMS_EPS) * g


def _params(sem, vmem=VMEM_LIMIT):
    return pltpu.CompilerParams(dimension_semantics=sem, vmem_limit_bytes=vmem)


def _cast_kernel(x_ref, o_ref):
    o_ref[...] = x_ref[...].astype(o_ref.dtype)


def cast_bf16(w):
    layers, k, n = w.shape
    rows = 1 << (min(k, CAST_BLOCK_ELEMS // n).bit_length() - 1)
    assert k % rows == 0
    spec = pl.BlockSpec((None, rows, n), lambda l, r: (l, r, 0))
    return pl.pallas_call(
        _cast_kernel,
        grid=(layers, k // rows),
        in_specs=[spec],
        out_specs=spec,
        out_shape=jax.ShapeDtypeStruct(w.shape, BF16),
        compiler_params=_params(("parallel", "parallel")),
        name="cast_bf16",
    )(w)


def _rms_norm_kernel(x_ref, g_ref, o_ref):
    o_ref[...] = _rms_rows(x_ref[...], g_ref[...]).astype(o_ref.dtype)


def rms_norm_bf16(x, g, *, tm=512):
    m, d = x.shape
    return pl.pallas_call(
        _rms_norm_kernel,
        grid=(m // tm,),
        in_specs=[pl.BlockSpec((tm, d), lambda i: (i, 0)), pl.BlockSpec((1, d), lambda i: (0, 0))],
        out_specs=pl.BlockSpec((tm, d), lambda i: (i, 0)),
        out_shape=jax.ShapeDtypeStruct((m, d), BF16),
        compiler_params=_params(("parallel",)),
        name="rms_norm_bf16",
    )(x, g.reshape(1, d))


def _proj_kernel(u_ref, w_ref, cs_ref, o_ref, wb_ref):
    @pl.when(pl.program_id(1) == 0)
    def _():
        wb_ref[...] = w_ref[...].astype(BF16)

    acc = jnp.dot(u_ref[...], wb_ref[...], preferred_element_type=F32)
    o_ref[...] = (acc * cs_ref[...]).astype(o_ref.dtype)


def proj_matmul(u, w, layer, col_scale, *, out_dtype=BF16, tm=1024, tn=1024):
    m, d = u.shape
    n = col_scale.shape[0]
    tn = min(tn, n)
    return pl.pallas_call(
        _proj_kernel,
        grid=(n // tn, m // tm),
        in_specs=[
            pl.BlockSpec((tm, d), lambda j, i: (i, 0)),
            pl.BlockSpec((None, d, tn), lambda j, i: (layer, 0, j)),
            pl.BlockSpec((1, tn), lambda j, i: (0, j)),
        ],
        out_specs=pl.BlockSpec((tm, tn), lambda j, i: (i, j)),
        out_shape=jax.ShapeDtypeStruct((m, n), out_dtype),
        scratch_shapes=[pltpu.VMEM((d, tn), BF16)],
        compiler_params=_params(("parallel", "arbitrary")),
        name="proj_matmul",
    )(u, w, col_scale.reshape(1, n))


def _matmul_norm_res_kernel(y_ref, w_ref, g_ref, res_ref, *rest, nk, has_next):
    if has_next:
        gn_ref, o_ref, u_ref = rest
    else:
        (o_ref,) = rest
    k = pl.program_id(1)

    def finish(acc):
        h = res_ref[...] + _rms_rows(acc, g_ref[...])
        o_ref[...] = h
        if has_next:
            u_ref[...] = _rms_rows(h, gn_ref[...]).astype(u_ref.dtype)

    if nk == 1:
        finish(jnp.dot(y_ref[...], w_ref[...], preferred_element_type=F32))
        return

    @pl.when(k == 0)
    def _():
        o_ref[...] = jnp.dot(y_ref[...], w_ref[...], preferred_element_type=F32)

    @pl.when(k > 0)
    def _():
        o_ref[...] += jnp.dot(y_ref[...], w_ref[...], preferred_element_type=F32)

    @pl.when(k == nk - 1)
    def _():
        finish(o_ref[...])


def matmul_norm_res(y, w, layer, g, res, g_next=None, *, tm=512, tk=2048):
    m, kdim = y.shape
    d = w.shape[2]
    nk = kdim // tk
    has_next = g_next is not None
    row_spec = pl.BlockSpec((tm, d), lambda i, k: (i, 0))
    gain_spec = pl.BlockSpec((1, d), lambda i, k: (0, 0))
    in_specs = [
        pl.BlockSpec((tm, tk), lambda i, k: (i, k)),
        pl.BlockSpec((None, tk, d), lambda i, k: (layer, k, 0)),
        gain_spec,
        row_spec,
    ]
    args = [y, w, g.reshape(1, d), res]
    out_shape = [jax.ShapeDtypeStruct((m, d), F32)]
    out_specs = [row_spec]
    if has_next:
        in_specs.append(gain_spec)
        args.append(g_next.reshape(1, d))
        out_shape.append(jax.ShapeDtypeStruct((m, d), BF16))
        out_specs.append(row_spec)
    outs = pl.pallas_call(
        functools.partial(_matmul_norm_res_kernel, nk=nk, has_next=has_next),
        grid=(m // tm, nk),
        in_specs=in_specs,
        out_specs=out_specs,
        out_shape=out_shape,
        compiler_params=_params(("parallel", "arbitrary")),
        name="matmul_norm_res",
    )(*args)
    return outs if has_next else (outs[0], None)


def _gelu_tanh(x):
    return x * (0.5 * (1.0 + jnp.tanh(math.sqrt(2.0 / math.pi) * (x + 0.044715 * (x * x * x)))))


def _causal_conv3(a, carry_ref, cw, cb):
    tm = a.shape[0]
    a_ext = jnp.concatenate([carry_ref[...], a], axis=0)
    carry_ref[...] = a[tm - CONV_HALO:, :]
    r1 = pltpu.roll(a_ext, 1, 0)
    r2 = pltpu.roll(a_ext, 2, 0)
    y = cw[0:1, :] * r2 + cw[1:2, :] * r1 + cw[2:3, :] * a_ext
    return y[CONV_HALO:, :] + cb


def _ffn_up_kernel(u_ref, wg_ref, wu_ref, cwg_ref, cwu_ref, cbg_ref, cbu_ref, o_ref,
                   wgb_ref, wub_ref, carry_ref, *, tm, seq):
    i = pl.program_id(1)

    @pl.when(i == 0)
    def _():
        wgb_ref[...] = wg_ref[...].astype(BF16)
        wub_ref[...] = wu_ref[...].astype(BF16)

    @pl.when((i * tm) % seq == 0)
    def _():
        carry_ref[...] = jnp.zeros(carry_ref.shape, F32)

    u = u_ref[...]
    ag = _causal_conv3(jnp.dot(u, wgb_ref[...], preferred_element_type=F32), carry_ref.at[0],
                       cwg_ref[...], cbg_ref[...])
    au = _causal_conv3(jnp.dot(u, wub_ref[...], preferred_element_type=F32), carry_ref.at[1],
                       cwu_ref[...], cbu_ref[...])
    o_ref[...] = (_gelu_tanh(ag) * au).astype(o_ref.dtype)


def ffn_up(u, w_up, layer, conv_w, conv_b, seq, *, tm=1024, tn=512):
    m, d = u.shape
    f = w_up.shape[2] // 2
    nn = f // tn
    cb = conv_b.reshape(1, 2 * f)
    return pl.pallas_call(
        functools.partial(_ffn_up_kernel, tm=tm, seq=seq),
        grid=(nn, m // tm),
        in_specs=[
            pl.BlockSpec((tm, d), lambda j, i: (i, 0)),
            pl.BlockSpec((None, d, tn), lambda j, i: (layer, 0, j)),
            pl.BlockSpec((None, d, tn), lambda j, i: (layer, 0, nn + j)),
            pl.BlockSpec((CONV_WIDTH, tn), lambda j, i: (0, j)),
            pl.BlockSpec((CONV_WIDTH, tn), lambda j, i: (0, nn + j)),
            pl.BlockSpec((1, tn), lambda j, i: (0, j)),
            pl.BlockSpec((1, tn), lambda j, i: (0, nn + j)),
        ],
        out_specs=pl.BlockSpec((tm, tn), lambda j, i: (i, j)),
        out_shape=jax.ShapeDtypeStruct((m, f), BF16),
        scratch_shapes=[pltpu.VMEM((d, tn), BF16), pltpu.VMEM((d, tn), BF16),
                        pltpu.VMEM((2, CONV_HALO, tn), F32)],
        compiler_params=_params(("parallel", "arbitrary")),
        name="ffn_up_conv_glu",
    )(u, w_up, w_up, conv_w, conv_w, cb, cb)


def _rel_bucket_np(dist):
    dist = np.maximum(dist, 0)
    max_exact = N_BUCKETS // 2
    d = np.maximum(dist, 1).astype(np.float32)
    large = max_exact + (np.log(d / max_exact) / math.log(MAX_DISTANCE / max_exact)
                         * (N_BUCKETS - max_exact)).astype(np.int32)
    large = np.minimum(large, N_BUCKETS - 1)
    return np.where(dist < max_exact, dist, large)


def _strip_distances(n_delta, t):
    return np.arange(n_delta)[:, None] * t + np.arange(2 * t)[None, :] - t


def _causal_log_mult(dist):
    return np.where(dist >= 0, 0.0, -np.inf)


def _dilated_log_mult(dist):
    mult = np.zeros(dist.shape, np.float64)
    for window, dil in DILATED_GROUPS:
        mult += (dist >= 0) & (dist <= window) & (dist % dil == 0)
    with np.errstate(divide="ignore"):
        return np.log(mult)


def _bias_strips(rel_bias, dist, log_mult):
    shown = np.isfinite(log_mult)
    bucket = _rel_bucket_np(dist)
    last_uniform = bool(shown[-1].all() and (log_mult[-1] == log_mult[-1, 0]).all()
                        and (bucket[-1] == bucket[-1, 0]).all())
    offset = jnp.asarray(np.where(shown, log_mult * LOG2E, NEG), F32)
    if rel_bias is None:
        return offset[None], last_uniform
    bias = jnp.take(rel_bias.astype(F32), jnp.asarray(bucket), axis=0)
    strips = jnp.where(jnp.asarray(shown)[..., None], bias * LOG2E + offset[..., None], NEG)
    return jnp.moveaxis(strips, -1, 0), last_uniform


def _attn_kernel(*refs, kind, g, t, seq, n_delta, last_uniform, lambda_init):
    n_in = {"dilated": 4, "moba": 4, "diff": 6, "fox": 6}[kind]
    q_ref, k_ref, v_ref, strip_ref = refs[:4]
    o_ref = refs[n_in]
    tab_ref, vt_ref, s_ref, p_ref = refs[n_in + 1:n_in + 5]
    hg = pl.program_id(0)
    nk = seq // t
    n_tab = tab_ref.shape[0]
    n_maps = s_ref.shape[0] // 2
    dh = HEAD_DIM

    @pl.when(pl.program_id(1) == 0)
    def _():
        for hh in range(n_tab):
            for dl in range(n_delta):
                strip = jnp.broadcast_to(strip_ref[hh, dl:dl + 1, :], (t, 2 * t))
                tab_ref[hh, dl] = pltpu.roll(strip, 0, 1, stride=1, stride_axis=0)[:, t:]

    def tile(j):
        return slice(j * t, (j + 1) * t)

    ones_row = lax.broadcasted_iota(jnp.int32, (DEN_ROWS, seq), 0) == 0
    vt_ref[dh:, :] = jnp.where(ones_row, 1.0, 0.0).astype(BF16)

    for hh in range(g):
        head = slice(hh * dh, (hh + 1) * dh)
        for jb in range(nk):
            vt_ref[:dh, tile(jb)] = v_ref[tile(jb), head].T
        if kind == "moba":
            kmean_ref = refs[n_in + 5]
            for jb in range(nk):
                kmean_ref[jb:jb + 1, :] = jnp.mean(k_ref[tile(jb), head].astype(F32), axis=0, keepdims=True)
            km = kmean_ref[...]
            km_hi = km.astype(BF16)
            km_lo = (km - km_hi.astype(F32)).astype(BF16)
        if kind == "fox":
            cs_ref, cst_ref, ckb_ref = refs[4], refs[5], refs[n_in + 5]
            cs = cs_ref[...]
            lane = lax.broadcasted_iota(jnp.int32, cs.shape, 1)
            ck = jnp.sum(jnp.where(lane == hg * g + hh, cs, 0.0), axis=-1, keepdims=True)
            ckb_ref[...] = jnp.broadcast_to(ck, cs.shape)
        if kind == "diff":
            lam_ref, sub_ref = refs[4], refs[5]
            lam = lam_ref[...]
            lmbda = (jnp.exp(jnp.sum(lam[0:1, :] * lam[1:2, :], axis=-1, keepdims=True))
                     - jnp.exp(jnp.sum(lam[2:3, :] * lam[3:4, :], axis=-1, keepdims=True)) + lambda_init)
        far_term = strip_ref[hh % n_tab, n_delta - 1:n_delta, 0:1] if last_uniform else None

        def is_far(i, j):
            return last_uniform and i - j >= n_delta - 1

        def score_pass(i):
            qt = q_ref[tile(i), head].T
            if kind == "diff":
                row = lax.broadcasted_iota(jnp.int32, qt.shape, 0)
                qts = [jnp.where(row < DIFF_HEAD_DIM, qt, jnp.zeros_like(qt)),
                       jnp.where(row >= DIFF_HEAD_DIM, qt, jnp.zeros_like(qt))]
            else:
                qts = [qt]
            sel = None
            if kind == "moba" and i > 0:
                gate = (jnp.dot(km_hi, qt, preferred_element_type=F32)
                        + jnp.dot(km_lo, qt, preferred_element_type=F32))
                blk = lax.broadcasted_iota(jnp.int32, gate.shape, 0)
                sel = []
                for jb in range(i):
                    gj = gate[jb:jb + 1, :]
                    beats = (blk < i) & ((gate > gj) | ((gate == gj) & (blk < jb)))
                    rank = jnp.sum(jnp.where(beats, 1.0, 0.0), axis=0, keepdims=True)
                    sel.append(jnp.where(rank < MOBA_TOPK, 0.0, NEG))
            maxima = []
            for mp in range(n_maps):
                buf = (i % 2) * n_maps + mp
                m = None
                for j in range(i + 1):
                    s = jnp.dot(k_ref[tile(j), head], qts[mp], preferred_element_type=F32)
                    if kind == "fox":
                        s = s + cst_ref[hh, i:i + 1, :] - jnp.concatenate([ckb_ref[tile(j), :]] * (t // dh), axis=1)
                    if not is_far(i, j):
                        s = s + tab_ref[hh % n_tab, min(i - j, n_delta - 1)]
                    if kind == "moba" and j < i:
                        s = s + sel[j]
                    s_ref[buf, tile(j), :] = s
                    tile_max = jnp.max(s, axis=0, keepdims=True)
                    if is_far(i, j):
                        tile_max = tile_max + far_term
                    m = tile_max if m is None else jnp.maximum(m, tile_max)
                maxima.append(m)
            return maxima

        def value_pass(i, maxima):
            keys = slice(0, (i + 1) * t)
            outs = []
            for mp in range(n_maps):
                buf = (i % 2) * n_maps + mp
                for j in range(i + 1):
                    shift = maxima[mp] - far_term if is_far(i, j) else maxima[mp]
                    p_ref[buf, tile(j), :] = jnp.exp2(s_ref[buf, tile(j), :] - shift).astype(BF16)
                acc = jnp.dot(vt_ref[:, keys], p_ref[buf, keys, :], preferred_element_type=F32)
                outs.append(acc[:dh, :] / acc[dh:dh + 1, :])
            if kind == "diff":
                out = _rms_rows((outs[0] - lmbda * outs[1]).T, sub_ref[...]) * (1.0 - lambda_init)
            else:
                out = outs[0].T
            o_ref[tile(i), head] = out.astype(o_ref.dtype)

        maxima = score_pass(0)
        for i in range(nk):
            next_maxima = score_pass(i + 1) if i + 1 < nk else None
            value_pass(i, maxima)
            maxima = next_maxima


def attention(kind, qkv, strips, last_uniform, batch, seq, extra=(), lambda_init=0.0):
    t, g, dh = ATTN_TILE, ATTN_HEADS_PER_STEP, HEAD_DIM
    ng = N_HEADS // g
    m = qkv.shape[0]
    per_head = strips.shape[0] == N_HEADS
    n_tab = g if per_head else 1
    n_delta = strips.shape[1]
    n_maps = 2 if kind == "diff" else 1
    in_specs = [
        pl.BlockSpec((seq, g * dh), lambda h, b: (b, h)),
        pl.BlockSpec((seq, g * dh), lambda h, b: (b, ng + h)),
        pl.BlockSpec((seq, g * dh), lambda h, b: (b, 2 * ng + h)),
        pl.BlockSpec((n_tab, n_delta, 2 * t), (lambda h, b: (h, 0, 0)) if per_head else (lambda h, b: (0, 0, 0))),
    ]
    args = [qkv, qkv, qkv, strips]
    scratch = [
        pltpu.VMEM((n_tab, n_delta, t, t), F32),
        pltpu.VMEM((dh + DEN_ROWS, seq), BF16),
        pltpu.VMEM((2 * n_maps, seq, t), F32),
        pltpu.VMEM((2 * n_maps, seq, t), BF16),
    ]
    if kind == "moba":
        scratch.append(pltpu.VMEM((seq // t, dh), F32))
    elif kind == "diff":
        lam, subln = extra
        in_specs += [pl.BlockSpec(lam.shape, lambda h, b: (0, 0)),
                     pl.BlockSpec((1, dh), lambda h, b: (0, 0))]
        args += [lam, subln.reshape(1, dh)]
    elif kind == "fox":
        csum, csum_t = extra
        in_specs += [pl.BlockSpec((None, seq, dh), lambda h, b: (b, 0, 0)),
                     pl.BlockSpec((None, g, seq // t, t), lambda h, b: (b, h, 0, 0))]
        args += [csum, csum_t]
        scratch.append(pltpu.VMEM((seq, dh), F32))
    return pl.pallas_call(
        functools.partial(_attn_kernel, kind=kind, g=g, t=t, seq=seq, n_delta=n_delta,
                          last_uniform=last_uniform, lambda_init=lambda_init),
        grid=(ng, batch),
        in_specs=in_specs,
        out_specs=pl.BlockSpec((seq, g * dh), lambda h, b: (b, h)),
        out_shape=jax.ShapeDtypeStruct((m, N_HEADS * dh), BF16),
        scratch_shapes=scratch,
        compiler_params=_params(("parallel", "arbitrary")),
        name=kind + "_attention",
    )(*args)


def _fox_gate_kernel(f_ref, b_ref, cs_ref, cst_ref):
    z = f_ref[...] + b_ref[...]
    x = jnp.minimum(z, 0.0) - jnp.log1p(jnp.exp(-jnp.abs(z)))
    row = lax.broadcasted_iota(jnp.int32, x.shape, 0)
    shift = 1
    while shift < x.shape[0]:
        x = x + jnp.where(row >= shift, pltpu.roll(x, shift, 0), 0.0)
        shift *= 2
    x = x * LOG2E
    cs_ref[...] = x
    cst_ref[...] = x.T


def fox_gate(f, b_pad):
    batch, seq, lanes = f.shape
    return pl.pallas_call(
        _fox_gate_kernel,
        grid=(batch,),
        in_specs=[pl.BlockSpec((None, seq, lanes), lambda b: (b, 0, 0)),
                  pl.BlockSpec((1, lanes), lambda b: (0, 0))],
        out_specs=[pl.BlockSpec((None, seq, lanes), lambda b: (b, 0, 0)),
                   pl.BlockSpec((None, lanes, seq), lambda b: (b, 0, 0))],
        out_shape=[jax.ShapeDtypeStruct((batch, seq, lanes), F32),
                   jax.ShapeDtypeStruct((batch, lanes, seq), F32)],
        compiler_params=_params(("parallel",)),
        name="fox_gate_cumsum",
    )(f, b_pad)


def kernel(x, rel_bias, norm_gains, w_out, w_up, conv_w, conv_b, w_down, w_in_dil, w_in_moba,
           w_in_diff, lambda_diff, subln_diff, w_in_fox, b_fox):
    batch, seq, d = x.shape
    depth = norm_gains.shape[0]
    t = ATTN_TILE
    assert seq % t == 0 and t == MOBA_BLOCK and d == N_HEADS * HEAD_DIM
    causal3 = _strip_distances(3, t)
    causal_strips, causal_uniform = _bias_strips(rel_bias, causal3, _causal_log_mult(causal3))

    def qkv_scale(head_scale):
        return jnp.concatenate([jnp.full((d,), head_scale * LOG2E, F32), jnp.ones((2 * d,), F32)])

    w_out_b, w_down_b = cast_bf16(w_out), cast_bf16(w_down)

    h = x.reshape(batch * seq, d)
    u = rms_norm_bf16(h, norm_gains[0, 0])
    for i in range(depth):
        mixer, j = i % N_MIXERS, i // N_MIXERS
        g = norm_gains[i]
        if mixer == 0:
            qkv = proj_matmul(u, w_in_dil, j, qkv_scale(HEAD_DIM ** -0.5))
            dist = _strip_distances(4, t)
            strips, uniform = _bias_strips(rel_bias, dist, _dilated_log_mult(dist))
            y = attention("dilated", qkv, strips, uniform, batch, seq)
        elif mixer == 1:
            qkv = proj_matmul(u, w_in_moba, j, qkv_scale(HEAD_DIM ** -0.5))
            y = attention("moba", qkv, causal_strips, causal_uniform, batch, seq)
        elif mixer == 2:
            qkv = proj_matmul(u, w_in_diff, j, qkv_scale(DIFF_HEAD_DIM ** -0.5))
            y = attention("diff", qkv, causal_strips, causal_uniform, batch, seq,
                          extra=(lambda_diff[j], subln_diff[j]), lambda_init=0.8 - 0.6 * math.exp(-0.3 * i))
        else:
            qkv = proj_matmul(u, w_in_fox, j, qkv_scale(HEAD_DIM ** -0.5))
            w_f = jnp.pad(w_in_fox[j, :, 3 * d:], ((0, 0), (0, HEAD_DIM - N_HEADS)))[None]
            f = proj_matmul(u, w_f, 0, jnp.ones((HEAD_DIM,), F32), out_dtype=F32)
            b_pad = jnp.pad(b_fox[j], (0, HEAD_DIM - N_HEADS)).reshape(1, HEAD_DIM)
            csum, csum_t = fox_gate(f.reshape(batch, seq, HEAD_DIM), b_pad)
            dist = _strip_distances(2, t)
            strips, uniform = _bias_strips(None, dist, _causal_log_mult(dist))
            y = attention("fox", qkv, strips, uniform, batch, seq,
                          extra=(csum, csum_t.reshape(batch, HEAD_DIM, seq // t, t)))
        h, u = matmul_norm_res(y, w_out_b, i, g[1], h, g[2], tk=d)
        hidden = ffn_up(u, w_up, i, conv_w[i], conv_b[i], seq)
        h, u = matmul_norm_res(hidden, w_down_b, i, g[3], h, norm_gains[i + 1, 0] if i + 1 < depth else None)
    return h.reshape(batch, seq, d)
```

```python
import functools
import math

import jax
import jax.numpy as jnp
import numpy as np
from jax import lax
from jax.experimental import pallas as pl
from jax.experimental.pallas import tpu as pltpu

N_HEADS = 16
HEAD_DIM = 128
N_MIXERS = 4
CONV_WIDTH = 3
RMS_EPS = 1e-6
N_BUCKETS = 32
MAX_DISTANCE = 128
DILATED_GROUPS = ((128, 1), (512, 4), (2048, 16))
MOBA_BLOCK = 256
MOBA_TOPK = 3
DIFF_HEAD_DIM = HEAD_DIM // 2
NEG = -1e30
LOG2E = math.log2(math.e)

ATTN_TILE = 256
ATTN_HEADS_PER_STEP = 2
DEN_ROWS = 16
CONV_HALO = 16
FINISH_ROW_CHUNK = 128
CAST_BLOCK_ELEMS = 2 * 1024 * 1024
VMEM_LIMIT = 52 * 1024 * 1024

BF16 = jnp.bfloat16
F32 = jnp.float32


def _rms_rows(x, g):
    return x * lax.rsqrt(jnp.mean(x * x, axis=-1, keepdims=True) + RMS_EPS) * g


def _params(sem, vmem=VMEM_LIMIT):
    return pltpu.CompilerParams(dimension_semantics=sem, vmem_limit_bytes=vmem)


def _cast_kernel(x_ref, o_ref):
    o_ref[...] = x_ref[...].astype(o_ref.dtype)


def cast_bf16(w):
    layers, k, n = w.shape
    rows = 1 << (min(k, CAST_BLOCK_ELEMS // n).bit_length() - 1)
    assert k % rows == 0
    spec = pl.BlockSpec((None, rows, n), lambda l, r: (l, r, 0))
    return pl.pallas_call(
        _cast_kernel,
        grid=(layers, k // rows),
        in_specs=[spec],
        out_specs=spec,
        out_shape=jax.ShapeDtypeStruct(w.shape, BF16),
        compiler_params=_params(("parallel", "parallel")),
        name="cast_bf16",
    )(w)


def _rms_norm_kernel(x_ref, g_ref, o_ref):
    o_ref[...] = _rms_rows(x_ref[...], g_ref[...]).astype(o_ref.dtype)


def rms_norm_bf16(x, g, *, tm=512):
    m, d = x.shape
    return pl.pallas_call(
        _rms_norm_kernel,
        grid=(m // tm,),
        in_specs=[pl.BlockSpec((tm, d), lambda i: (i, 0)), pl.BlockSpec((1, d), lambda i: (0, 0))],
        out_specs=pl.BlockSpec((tm, d), lambda i: (i, 0)),
        out_shape=jax.ShapeDtypeStruct((m, d), BF16),
        compiler_params=_params(("parallel",)),
        name="rms_norm_bf16",
    )(x, g.reshape(1, d))


def _proj_kernel(u_ref, w_ref, cs_ref, o_ref, wb_ref):
    @pl.when(pl.program_id(1) == 0)
    def _():
        wb_ref[...] = w_ref[...].astype(BF16)

    acc = jnp.dot(u_ref[...], wb_ref[...], preferred_element_type=F32)
    o_ref[...] = (acc * cs_ref[...]).astype(o_ref.dtype)


def proj_matmul(u, w, layer, col_scale, *, out_dtype=BF16, tm=1024, tn=1024):
    m, d = u.shape
    n = col_scale.shape[0]
    tn = min(tn, n)
    return pl.pallas_call(
        _proj_kernel,
        grid=(n // tn, m // tm),
        in_specs=[
            pl.BlockSpec((tm, d), lambda j, i: (i, 0)),
            pl.BlockSpec((None, d, tn), lambda j, i: (layer, 0, j)),
            pl.BlockSpec((1, tn), lambda j, i: (0, j)),
        ],
        out_specs=pl.BlockSpec((tm, tn), lambda j, i: (i, j)),
        out_shape=jax.ShapeDtypeStruct((m, n), out_dtype),
        scratch_shapes=[pltpu.VMEM((d, tn), BF16)],
        compiler_params=_params(("parallel", "arbitrary")),
        name="proj_matmul",
    )(u, w, col_scale.reshape(1, n))


def _matmul_norm_res_kernel(y_ref, w_ref, g_ref, res_ref, *rest, nk, has_next):
    if has_next:
        gn_ref, o_ref, u_ref = rest
    else:
        (o_ref,) = rest
    k = pl.program_id(1)

    def last_step(first):
        for r in range(0, o_ref.shape[0], FINISH_ROW_CHUNK):
            rows = slice(r, r + FINISH_ROW_CHUNK)
            acc = jnp.dot(y_ref[rows, :], w_ref[...], preferred_element_type=F32)
            if not first:
                acc = acc + o_ref[rows, :]
            h = res_ref[rows, :] + _rms_rows(acc, g_ref[...])
            o_ref[rows, :] = h
            if has_next:
                u_ref[rows, :] = _rms_rows(h, gn_ref[...]).astype(u_ref.dtype)

    if nk == 1:
        last_step(True)
        return

    @pl.when(k == 0)
    def _():
        o_ref[...] = jnp.dot(y_ref[...], w_ref[...], preferred_element_type=F32)

    @pl.when((k > 0) & (k < nk - 1))
    def _():
        o_ref[...] += jnp.dot(y_ref[...], w_ref[...], preferred_element_type=F32)

    @pl.when(k == nk - 1)
    def _():
        last_step(False)


def matmul_norm_res(y, w, layer, g, res, g_next=None, *, tm=512, tk=2048):
    m, kdim = y.shape
    d = w.shape[2]
    nk = kdim // tk
    has_next = g_next is not None
    row_spec = pl.BlockSpec((tm, d), lambda i, k: (i, 0))
    gain_spec = pl.BlockSpec((1, d), lambda i, k: (0, 0))
    in_specs = [
        pl.BlockSpec((tm, tk), lambda i, k: (i, k)),
        pl.BlockSpec((None, tk, d), lambda i, k: (layer, k, 0)),
        gain_spec,
        row_spec,
    ]
    args = [y, w, g.reshape(1, d), res]
    out_shape = [jax.ShapeDtypeStruct((m, d), F32)]
    out_specs = [row_spec]
    if has_next:
        in_specs.append(gain_spec)
        args.append(g_next.reshape(1, d))
        out_shape.append(jax.ShapeDtypeStruct((m, d), BF16))
        out_specs.append(row_spec)
    outs = pl.pallas_call(
        functools.partial(_matmul_norm_res_kernel, nk=nk, has_next=has_next),
        grid=(m // tm, nk),
        in_specs=in_specs,
        out_specs=out_specs,
        out_shape=out_shape,
        compiler_params=_params(("parallel", "arbitrary")),
        name="matmul_norm_res",
    )(*args)
    return outs if has_next else (outs[0], None)


def _gelu_tanh(x):
    return x * (0.5 * (1.0 + jnp.tanh(math.sqrt(2.0 / math.pi) * (x + 0.044715 * (x * x * x)))))


def _causal_conv3(a, tail, cw, cb):
    a_ext = jnp.concatenate([tail, a], axis=0)
    r1 = pltpu.roll(a_ext, 1, 0)
    r2 = pltpu.roll(a_ext, 2, 0)
    y = cw[0:1, :] * r2 + cw[1:2, :] * r1 + cw[2:3, :] * a_ext
    return y[CONV_HALO:, :] + cb


def _ffn_up_kernel(u_ref, wg_ref, wu_ref, cwg_ref, cwu_ref, cbg_ref, cbu_ref, o_ref,
                   wgb_ref, wub_ref, carry_ref, *, tm, seq):
    i = pl.program_id(1)

    @pl.when(i == 0)
    def _():
        wgb_ref[...] = wg_ref[...].astype(BF16)
        wub_ref[...] = wu_ref[...].astype(BF16)

    @pl.when((i * tm) % seq == 0)
    def _():
        carry_ref[...] = jnp.zeros(carry_ref.shape, F32)

    u = u_ref[...]
    act = []
    for idx, (wb_ref, cw_ref, cb_ref) in enumerate(((wgb_ref, cwg_ref, cbg_ref), (wub_ref, cwu_ref, cbu_ref))):
        a = jnp.dot(u, wb_ref[...], preferred_element_type=F32)
        act.append(_causal_conv3(a, carry_ref[idx], cw_ref[...], cb_ref[...]))
        carry_ref[idx] = a[tm - CONV_HALO:, :]
    o_ref[...] = (_gelu_tanh(act[0]) * act[1]).astype(o_ref.dtype)


def ffn_up(u, w_up, layer, conv_w, conv_b, seq, *, tm=1024, tn=512):
    m, d = u.shape
    f = w_up.shape[2] // 2
    nn = f // tn
    cb = conv_b.reshape(1, 2 * f)
    return pl.pallas_call(
        functools.partial(_ffn_up_kernel, tm=tm, seq=seq),
        grid=(nn, m // tm),
        in_specs=[
            pl.BlockSpec((tm, d), lambda j, i: (i, 0)),
            pl.BlockSpec((None, d, tn), lambda j, i: (layer, 0, j)),
            pl.BlockSpec((None, d, tn), lambda j, i: (layer, 0, nn + j)),
            pl.BlockSpec((CONV_WIDTH, tn), lambda j, i: (0, j)),
            pl.BlockSpec((CONV_WIDTH, tn), lambda j, i: (0, nn + j)),
            pl.BlockSpec((1, tn), lambda j, i: (0, j)),
            pl.BlockSpec((1, tn), lambda j, i: (0, nn + j)),
        ],
        out_specs=pl.BlockSpec((tm, tn), lambda j, i: (i, j)),
        out_shape=jax.ShapeDtypeStruct((m, f), BF16),
        scratch_shapes=[pltpu.VMEM((d, tn), BF16), pltpu.VMEM((d, tn), BF16),
                        pltpu.VMEM((2, CONV_HALO, tn), F32)],
        compiler_params=_params(("parallel", "arbitrary")),
        name="ffn_up_conv_glu",
    )(u, w_up, w_up, conv_w, conv_w, cb, cb)


def _rel_bucket_np(dist):
    dist = np.maximum(dist, 0)
    max_exact = N_BUCKETS // 2
    d = np.maximum(dist, 1).astype(np.float32)
    large = max_exact + (np.log(d / max_exact) / math.log(MAX_DISTANCE / max_exact)
                         * (N_BUCKETS - max_exact)).astype(np.int32)
    large = np.minimum(large, N_BUCKETS - 1)
    return np.where(dist < max_exact, dist, large)


def _strip_distances(n_delta, t):
    return np.arange(n_delta)[:, None] * t + np.arange(2 * t)[None, :] - t


def _causal_log_mult(dist):
    return np.where(dist >= 0, 0.0, -np.inf)


def _dilated_log_mult(dist):
    mult = np.zeros(dist.shape, np.float64)
    for window, dil in DILATED_GROUPS:
        mult += (dist >= 0) & (dist <= window) & (dist % dil == 0)
    with np.errstate(divide="ignore"):
        return np.log(mult)


def _bias_strips(rel_bias, dist, log_mult):
    shown = np.isfinite(log_mult)
    bucket = _rel_bucket_np(dist)
    last_uniform = bool(shown[-1].all() and (log_mult[-1] == log_mult[-1, 0]).all()
                        and (bucket[-1] == bucket[-1, 0]).all())
    offset = jnp.asarray(np.where(shown, log_mult * LOG2E, NEG), F32)
    if rel_bias is None:
        return offset[None], last_uniform
    bias = jnp.take(rel_bias.astype(F32), jnp.asarray(bucket), axis=0)
    strips = jnp.where(jnp.asarray(shown)[..., None], bias * LOG2E + offset[..., None], NEG)
    return jnp.moveaxis(strips, -1, 0), last_uniform


def _attn_kernel(*refs, kind, g, t, seq, n_delta, last_uniform, lambda_init):
    n_in = {"dilated": 4, "moba": 4, "diff": 6, "fox": 6}[kind]
    q_ref, k_ref, v_ref, strip_ref = refs[:4]
    o_ref = refs[n_in]
    tab_ref, vt_ref, s_ref, p_ref = refs[n_in + 1:n_in + 5]
    hg = pl.program_id(0)
    nk = seq // t
    n_tab = tab_ref.shape[0]
    n_maps = s_ref.shape[0] // (2 * g)
    dh = HEAD_DIM

    @pl.when(pl.program_id(1) == 0)
    def _():
        for hh in range(n_tab):
            for dl in range(n_delta):
                strip = jnp.broadcast_to(strip_ref[hh, dl:dl + 1, :], (t, 2 * t))
                tab_ref[hh, dl] = pltpu.roll(strip, 0, 1, stride=1, stride_axis=0)[:, t:]

    def tile(j):
        return slice(j * t, (j + 1) * t)

    def head_passes(hh):
        head = slice(hh * dh, (hh + 1) * dh)
        ones_row = lax.broadcasted_iota(jnp.int32, (DEN_ROWS, seq), 0) == 0
        vt_ref[hh, dh:, :] = jnp.where(ones_row, 1.0, 0.0).astype(BF16)
        for jb in range(nk):
            vt_ref[hh, :dh, tile(jb)] = v_ref[tile(jb), head].T
        if kind == "moba":
            kmean_ref = refs[n_in + 5]
            for jb in range(nk):
                kmean_ref[hh, jb:jb + 1, :] = jnp.mean(k_ref[tile(jb), head].astype(F32), axis=0, keepdims=True)
            km = kmean_ref[hh]
            km_hi = km.astype(BF16)
            km_lo = (km - km_hi.astype(F32)).astype(BF16)
        if kind == "fox":
            cs_ref, cst_ref, ckb_ref = refs[4], refs[5], refs[n_in + 5]
            cs = cs_ref[...]
            lane = lax.broadcasted_iota(jnp.int32, cs.shape, 1)
            ck = jnp.sum(jnp.where(lane == hg * g + hh, cs, 0.0), axis=-1, keepdims=True)
            ckb_ref[hh] = jnp.broadcast_to(ck, cs.shape)
        if kind == "diff":
            lam_ref, sub_ref = refs[4], refs[5]
            lam = lam_ref[...]
            lmbda = (jnp.exp(jnp.sum(lam[0:1, :] * lam[1:2, :], axis=-1, keepdims=True))
                     - jnp.exp(jnp.sum(lam[2:3, :] * lam[3:4, :], axis=-1, keepdims=True)) + lambda_init)
        far_term = strip_ref[hh % n_tab, n_delta - 1:n_delta, 0:1] if last_uniform else None

        def is_far(i, j):
            return last_uniform and i - j >= n_delta - 1

        def score_pass(i):
            qt = q_ref[tile(i), head].T
            if kind == "diff":
                row = lax.broadcasted_iota(jnp.int32, qt.shape, 0)
                qts = [jnp.where(row < DIFF_HEAD_DIM, qt, jnp.zeros_like(qt)),
                       jnp.where(row >= DIFF_HEAD_DIM, qt, jnp.zeros_like(qt))]
            else:
                qts = [qt]
            sel = None
            if kind == "moba" and i > 0:
                gate = (jnp.dot(km_hi, qt, preferred_element_type=F32)
                        + jnp.dot(km_lo, qt, preferred_element_type=F32))
                blk = lax.broadcasted_iota(jnp.int32, gate.shape, 0)
                sel = []
                for jb in range(i):
                    gj = gate[jb:jb + 1, :]
                    beats = (blk < i) & ((gate > gj) | ((gate == gj) & (blk < jb)))
                    rank = jnp.sum(jnp.where(beats, 1.0, 0.0), axis=0, keepdims=True)
                    sel.append(jnp.where(rank < MOBA_TOPK, 0.0, NEG))
            maxima = []
            for mp in range(n_maps):
                buf = (hh * 2 + i % 2) * n_maps + mp
                m = None
                for j in range(i + 1):
                    s = jnp.dot(k_ref[tile(j), head], qts[mp], preferred_element_type=F32)
                    if kind == "fox":
                        s = s + cst_ref[hh, i:i + 1, :] - jnp.concatenate([ckb_ref[hh, tile(j), :]] * (t // dh), axis=1)
                    if not is_far(i, j):
                        s = s + tab_ref[hh % n_tab, min(i - j, n_delta - 1)]
                    if kind == "moba" and j < i:
                        s = s + sel[j]
                    s_ref[buf, tile(j), :] = s
                    tile_max = jnp.max(s, axis=0, keepdims=True)
                    if is_far(i, j):
                        tile_max = tile_max + far_term
                    m = tile_max if m is None else jnp.maximum(m, tile_max)
                maxima.append(m)
            return maxima

        def value_pass(i, maxima):
            keys = slice(0, (i + 1) * t)
            outs = []
            for mp in range(n_maps):
                buf = (hh * 2 + i % 2) * n_maps + mp
                for j in range(i + 1):
                    shift = maxima[mp] - far_term if is_far(i, j) else maxima[mp]
                    p_ref[buf, tile(j), :] = jnp.exp2(s_ref[buf, tile(j), :] - shift).astype(BF16)
                acc = jnp.dot(vt_ref[hh, :, keys], p_ref[buf, keys, :], preferred_element_type=F32)
                outs.append(acc[:dh, :] / acc[dh:dh + 1, :])
            if kind == "diff":
                out = _rms_rows((outs[0] - lmbda * outs[1]).T, sub_ref[...]) * (1.0 - lambda_init)
            else:
                out = outs[0].T
            o_ref[tile(i), head] = out.astype(o_ref.dtype)

        return score_pass, value_pass

    passes = [head_passes(hh) for hh in range(g)]
    maxima = [score_pass(0) for score_pass, _ in passes]
    for i in range(nk):
        next_maxima = [score_pass(i + 1) for score_pass, _ in passes] if i + 1 < nk else None
        for (_, value_pass), head_maxima in zip(passes, maxima):
            value_pass(i, head_maxima)
        maxima = next_maxima


def attention(kind, qkv, strips, last_uniform, batch, seq, extra=(), lambda_init=0.0):
    t, g, dh = ATTN_TILE, ATTN_HEADS_PER_STEP, HEAD_DIM
    ng = N_HEADS // g
    m = qkv.shape[0]
    per_head = strips.shape[0] == N_HEADS
    n_tab = g if per_head else 1
    n_delta = strips.shape[1]
    n_maps = 2 if kind == "diff" else 1
    in_specs = [
        pl.BlockSpec((seq, g * dh), lambda h, b: (b, h)),
        pl.BlockSpec((seq, g * dh), lambda h, b: (b, ng + h)),
        pl.BlockSpec((seq, g * dh), lambda h, b: (b, 2 * ng + h)),
        pl.BlockSpec((n_tab, n_delta, 2 * t), (lambda h, b: (h, 0, 0)) if per_head else (lambda h, b: (0, 0, 0))),
    ]
    args = [qkv, qkv, qkv, strips]
    scratch = [
        pltpu.VMEM((n_tab, n_delta, t, t), F32),
        pltpu.VMEM((g, dh + DEN_ROWS, seq), BF16),
        pltpu.VMEM((g * 2 * n_maps, seq, t), F32),
        pltpu.VMEM((g * 2 * n_maps, seq, t), BF16),
    ]
    if kind == "moba":
        scratch.append(pltpu.VMEM((g, seq // t, dh), F32))
    elif kind == "diff":
        lam, subln = extra
        in_specs += [pl.BlockSpec(lam.shape, lambda h, b: (0, 0)),
                     pl.BlockSpec((1, dh), lambda h, b: (0, 0))]
        args += [lam, subln.reshape(1, dh)]
    elif kind == "fox":
        csum, csum_t = extra
        in_specs += [pl.BlockSpec((None, seq, dh), lambda h, b: (b, 0, 0)),
                     pl.BlockSpec((None, g, seq // t, t), lambda h, b: (b, h, 0, 0))]
        args += [csum, csum_t]
        scratch.append(pltpu.VMEM((g, seq, dh), F32))
    return pl.pallas_call(
        functools.partial(_attn_kernel, kind=kind, g=g, t=t, seq=seq, n_delta=n_delta,
                          last_uniform=last_uniform, lambda_init=lambda_init),
        grid=(ng, batch),
        in_specs=in_specs,
        out_specs=pl.BlockSpec((seq, g * dh), lambda h, b: (b, h)),
        out_shape=jax.ShapeDtypeStruct((m, N_HEADS * dh), BF16),
        scratch_shapes=scratch,
        compiler_params=_params(("parallel", "arbitrary")),
        name=kind + "_attention",
    )(*args)


def _fox_gate_kernel(f_ref, b_ref, cs_ref, cst_ref):
    z = f_ref[...] + b_ref[...]
    x = jnp.minimum(z, 0.0) - jnp.log1p(jnp.exp(-jnp.abs(z)))
    row = lax.broadcasted_iota(jnp.int32, x.shape, 0)
    shift = 1
    while shift < x.shape[0]:
        x = x + jnp.where(row >= shift, pltpu.roll(x, shift, 0), 0.0)
        shift *= 2
    x = x * LOG2E
    cs_ref[...] = x
    cst_ref[...] = x.T


def fox_gate(f, b_pad):
    batch, seq, lanes = f.shape
    return pl.pallas_call(
        _fox_gate_kernel,
        grid=(batch,),
        in_specs=[pl.BlockSpec((None, seq, lanes), lambda b: (b, 0, 0)),
                  pl.BlockSpec((1, lanes), lambda b: (0, 0))],
        out_specs=[pl.BlockSpec((None, seq, lanes), lambda b: (b, 0, 0)),
                   pl.BlockSpec((None, lanes, seq), lambda b: (b, 0, 0))],
        out_shape=[jax.ShapeDtypeStruct((batch, seq, lanes), F32),
                   jax.ShapeDtypeStruct((batch, lanes, seq), F32)],
        compiler_params=_params(("parallel",)),
        name="fox_gate_cumsum",
    )(f, b_pad)


def kernel(x, rel_bias, norm_gains, w_out, w_up, conv_w, conv_b, w_down, w_in_dil, w_in_moba,
           w_in_diff, lambda_diff, subln_diff, w_in_fox, b_fox):
    batch, seq, d = x.shape
    depth = norm_gains.shape[0]
    t = ATTN_TILE
    assert seq % t == 0 and t == MOBA_BLOCK and d == N_HEADS * HEAD_DIM
    causal3 = _strip_distances(3, t)
    causal_strips, causal_uniform = _bias_strips(rel_bias, causal3, _causal_log_mult(causal3))

    def qkv_scale(head_scale):
        return jnp.concatenate([jnp.full((d,), head_scale * LOG2E, F32), jnp.ones((2 * d,), F32)])

    w_out_b, w_down_b = cast_bf16(w_out), cast_bf16(w_down)

    h = x.reshape(batch * seq, d)
    u = rms_norm_bf16(h, norm_gains[0, 0])
    for i in range(depth):
        mixer, j = i % N_MIXERS, i // N_MIXERS
        g = norm_gains[i]
        if mixer == 0:
            qkv = proj_matmul(u, w_in_dil, j, qkv_scale(HEAD_DIM ** -0.5))
            dist = _strip_distances(4, t)
            strips, uniform = _bias_strips(rel_bias, dist, _dilated_log_mult(dist))
            y = attention("dilated", qkv, strips, uniform, batch, seq)
        elif mixer == 1:
            qkv = proj_matmul(u, w_in_moba, j, qkv_scale(HEAD_DIM ** -0.5))
            y = attention("moba", qkv, causal_strips, causal_uniform, batch, seq)
        elif mixer == 2:
            qkv = proj_matmul(u, w_in_diff, j, qkv_scale(DIFF_HEAD_DIM ** -0.5))
            y = attention("diff", qkv, causal_strips, causal_uniform, batch, seq,
                          extra=(lambda_diff[j], subln_diff[j]), lambda_init=0.8 - 0.6 * math.exp(-0.3 * i))
        else:
            qkv = proj_matmul(u, w_in_fox, j, qkv_scale(HEAD_DIM ** -0.5))
            w_f = jnp.pad(w_in_fox[j, :, 3 * d:], ((0, 0), (0, HEAD_DIM - N_HEADS)))[None]
            f = proj_matmul(u, w_f, 0, jnp.ones((HEAD_DIM,), F32), out_dtype=F32)
            b_pad = jnp.pad(b_fox[j], (0, HEAD_DIM - N_HEADS)).reshape(1, HEAD_DIM)
            csum, csum_t = fox_gate(f.reshape(batch, seq, HEAD_DIM), b_pad)
            dist = _strip_distances(2, t)
            strips, uniform = _bias_strips(None, dist, _causal_log_mult(dist))
            y = attention("fox", qkv, strips, uniform, batch, seq,
                          extra=(csum, csum_t.reshape(batch, HEAD_DIM, seq // t, t)))
        h, u = matmul_norm_res(y, w_out_b, i, g[1], h, g[2], tk=d)
        hidden = ffn_up(u, w_up, i, conv_w[i], conv_b[i], seq)
        h, u = matmul_norm_res(hidden, w_down_b, i, g[3], h, norm_gains[i + 1, 0] if i + 1 < depth else None)
    return h.reshape(batch, seq, d)
```

```python
import functools
import math

import jax
import jax.numpy as jnp
import numpy as np
from jax import lax
from jax.experimental import pallas as pl
from jax.experimental.pallas import tpu as pltpu

N_HEADS = 16
HEAD_DIM = 128
N_MIXERS = 4
CONV_WIDTH = 3
RMS_EPS = 1e-6
N_BUCKETS = 32
MAX_DISTANCE = 128
DILATED_GROUPS = ((128, 1), (512, 4), (2048, 16))
MOBA_BLOCK = 256
MOBA_TOPK = 3
DIFF_HEAD_DIM = HEAD_DIM // 2
NEG = -1e30
LOG2E = math.log2(math.e)

ATTN_TILE = 256
ATTN_HEADS_PER_STEP = 2
DEN_ROWS = 16
CONV_HALO = 16
FINISH_ROW_CHUNK = 128
ACC_ROW_CHUNK = 512
CAST_BLOCK_ELEMS = 2 * 1024 * 1024
VMEM_LIMIT = 52 * 1024 * 1024

BF16 = jnp.bfloat16
F32 = jnp.float32


def _rms_rows(x, g):
    return x * lax.rsqrt(jnp.mean(x * x, axis=-1, keepdims=True) + RMS_EPS) * g


def _params(sem, vmem=VMEM_LIMIT):
    return pltpu.CompilerParams(dimension_semantics=sem, vmem_limit_bytes=vmem)


def _cast_kernel(x_ref, o_ref):
    o_ref[...] = x_ref[...].astype(o_ref.dtype)


def cast_bf16(w):
    layers, k, n = w.shape
    rows = 1 << (min(k, CAST_BLOCK_ELEMS // n).bit_length() - 1)
    assert k % rows == 0
    spec = pl.BlockSpec((None, rows, n), lambda l, r: (l, r, 0))
    return pl.pallas_call(
        _cast_kernel,
        grid=(layers, k // rows),
        in_specs=[spec],
        out_specs=spec,
        out_shape=jax.ShapeDtypeStruct(w.shape, BF16),
        compiler_params=_params(("parallel", "parallel")),
        name="cast_bf16",
    )(w)


def _rms_norm_kernel(x_ref, g_ref, o_ref):
    o_ref[...] = _rms_rows(x_ref[...], g_ref[...]).astype(o_ref.dtype)


def rms_norm_bf16(x, g, *, tm=512):
    m, d = x.shape
    return pl.pallas_call(
        _rms_norm_kernel,
        grid=(m // tm,),
        in_specs=[pl.BlockSpec((tm, d), lambda i: (i, 0)), pl.BlockSpec((1, d), lambda i: (0, 0))],
        out_specs=pl.BlockSpec((tm, d), lambda i: (i, 0)),
        out_shape=jax.ShapeDtypeStruct((m, d), BF16),
        compiler_params=_params(("parallel",)),
        name="rms_norm_bf16",
    )(x, g.reshape(1, d))


def _proj_kernel(u_ref, w_ref, cs_ref, o_ref, wb_ref, *, w_transposed):
    @pl.when(pl.program_id(1) == 0)
    def _():
        wb = w_ref[...].astype(BF16)
        wb_ref[...] = wb.T if w_transposed else wb

    acc = jnp.dot(u_ref[...], wb_ref[...], preferred_element_type=F32)
    o_ref[...] = (acc * cs_ref[...]).astype(o_ref.dtype)


def proj_matmul(u, w, layer, col_scale, *, w_transposed=False, out_dtype=BF16, tm=1024, tn=1024):
    m, d = u.shape
    n = col_scale.shape[0]
    tn = min(tn, n)
    w_spec = (pl.BlockSpec((None, tn, d), lambda j, i: (layer, j, 0)) if w_transposed
              else pl.BlockSpec((None, d, tn), lambda j, i: (layer, 0, j)))
    return pl.pallas_call(
        functools.partial(_proj_kernel, w_transposed=w_transposed),
        grid=(n // tn, m // tm),
        in_specs=[
            pl.BlockSpec((tm, d), lambda j, i: (i, 0)),
            w_spec,
            pl.BlockSpec((1, tn), lambda j, i: (0, j)),
        ],
        out_specs=pl.BlockSpec((tm, tn), lambda j, i: (i, j)),
        out_shape=jax.ShapeDtypeStruct((m, n), out_dtype),
        scratch_shapes=[pltpu.VMEM((d, tn), BF16)],
        compiler_params=_params(("parallel", "arbitrary")),
        name="proj_matmul",
    )(u, w, col_scale.reshape(1, n))


def _matmul_norm_res_kernel(y_ref, w_ref, g_ref, res_ref, *rest, nk, has_next):
    if has_next:
        gn_ref, o_ref, u_ref = rest
    else:
        (o_ref,) = rest
    k = pl.program_id(1)

    def last_step(first):
        for r in range(0, o_ref.shape[0], FINISH_ROW_CHUNK):
            rows = slice(r, r + FINISH_ROW_CHUNK)
            acc = jnp.dot(y_ref[rows, :], w_ref[...], preferred_element_type=F32)
            if not first:
                acc = acc + o_ref[rows, :]
            h = res_ref[rows, :] + _rms_rows(acc, g_ref[...])
            o_ref[rows, :] = h
            if has_next:
                u_ref[rows, :] = _rms_rows(h, gn_ref[...]).astype(u_ref.dtype)

    if nk == 1:
        last_step(True)
        return

    @pl.when(k == 0)
    def _():
        o_ref[...] = jnp.dot(y_ref[...], w_ref[...], preferred_element_type=F32)

    @pl.when((k > 0) & (k < nk - 1))
    def _():
        for r in range(0, o_ref.shape[0], ACC_ROW_CHUNK):
            rows = slice(r, r + ACC_ROW_CHUNK)
            o_ref[rows, :] += jnp.dot(y_ref[rows, :], w_ref[...], preferred_element_type=F32)

    @pl.when(k == nk - 1)
    def _():
        last_step(False)


def matmul_norm_res(y, w, layer, g, res, g_next=None, *, tm=512, tk=2048, single_buffer_res=False):
    m, kdim = y.shape
    d = w.shape[2]
    nk = kdim // tk
    has_next = g_next is not None
    row_spec = pl.BlockSpec((tm, d), lambda i, k: (i, 0))
    gain_spec = pl.BlockSpec((1, d), lambda i, k: (0, 0))
    res_spec = pl.BlockSpec((tm, d), lambda i, k: (i, 0), pipeline_mode=pl.Buffered(1)) if single_buffer_res else row_spec
    in_specs = [
        pl.BlockSpec((tm, tk), lambda i, k: (i, k)),
        pl.BlockSpec((None, tk, d), lambda i, k: (layer, k, 0)),
        gain_spec,
        res_spec,
    ]
    args = [y, w, g.reshape(1, d), res]
    out_shape = [jax.ShapeDtypeStruct((m, d), F32)]
    out_specs = [row_spec]
    if has_next:
        in_specs.append(gain_spec)
        args.append(g_next.reshape(1, d))
        out_shape.append(jax.ShapeDtypeStruct((m, d), BF16))
        out_specs.append(row_spec)
    outs = pl.pallas_call(
        functools.partial(_matmul_norm_res_kernel, nk=nk, has_next=has_next),
        grid=(m // tm, nk),
        in_specs=in_specs,
        out_specs=out_specs,
        out_shape=out_shape,
        compiler_params=_params(("parallel", "arbitrary")),
        name="matmul_norm_res",
    )(*args)
    return outs if has_next else (outs[0], None)


def _gelu_tanh(x):
    return x * (0.5 * (1.0 + jnp.tanh(math.sqrt(2.0 / math.pi) * (x + 0.044715 * (x * x * x)))))


def _causal_conv3(a, tail, cw, cb):
    a_ext = jnp.concatenate([tail, a], axis=0)
    r1 = pltpu.roll(a_ext, 1, 0)
    r2 = pltpu.roll(a_ext, 2, 0)
    y = cw[0:1, :] * r2 + cw[1:2, :] * r1 + cw[2:3, :] * a_ext
    return y[CONV_HALO:, :] + cb


def _ffn_up_kernel(u_ref, wg_ref, wu_ref, cwg_ref, cwu_ref, cbg_ref, cbu_ref, o_ref,
                   wgb_ref, wub_ref, carry_ref, *, tm, seq):
    i = pl.program_id(1)

    @pl.when(i == 0)
    def _():
        wgb_ref[...] = wg_ref[...].astype(BF16)
        wub_ref[...] = wu_ref[...].astype(BF16)

    @pl.when((i * tm) % seq == 0)
    def _():
        carry_ref[...] = jnp.zeros(carry_ref.shape, F32)

    u = u_ref[...]
    act = []
    for idx, (wb_ref, cw_ref, cb_ref) in enumerate(((wgb_ref, cwg_ref, cbg_ref), (wub_ref, cwu_ref, cbu_ref))):
        a = jnp.dot(u, wb_ref[...], preferred_element_type=F32)
        act.append(_causal_conv3(a, carry_ref[idx], cw_ref[...], cb_ref[...]))
        carry_ref[idx] = a[tm - CONV_HALO:, :]
    o_ref[...] = (_gelu_tanh(act[0]) * act[1]).astype(o_ref.dtype)


def ffn_up(u, w_up, layer, conv_w, conv_b, seq, *, tm=1024, tn=512):
    m, d = u.shape
    f = w_up.shape[2] // 2
    nn = f // tn
    cb = conv_b.reshape(1, 2 * f)
    return pl.pallas_call(
        functools.partial(_ffn_up_kernel, tm=tm, seq=seq),
        grid=(nn, m // tm),
        in_specs=[
            pl.BlockSpec((tm, d), lambda j, i: (i, 0)),
            pl.BlockSpec((None, d, tn), lambda j, i: (layer, 0, j)),
            pl.BlockSpec((None, d, tn), lambda j, i: (layer, 0, nn + j)),
            pl.BlockSpec((CONV_WIDTH, tn), lambda j, i: (0, j)),
            pl.BlockSpec((CONV_WIDTH, tn), lambda j, i: (0, nn + j)),
            pl.BlockSpec((1, tn), lambda j, i: (0, j)),
            pl.BlockSpec((1, tn), lambda j, i: (0, nn + j)),
        ],
        out_specs=pl.BlockSpec((tm, tn), lambda j, i: (i, j)),
        out_shape=jax.ShapeDtypeStruct((m, f), BF16),
        scratch_shapes=[pltpu.VMEM((d, tn), BF16), pltpu.VMEM((d, tn), BF16),
                        pltpu.VMEM((2, CONV_HALO, tn), F32)],
        compiler_params=_params(("parallel", "arbitrary")),
        name="ffn_up_conv_glu",
    )(u, w_up, w_up, conv_w, conv_w, cb, cb)


def _rel_bucket_np(dist):
    dist = np.maximum(dist, 0)
    max_exact = N_BUCKETS // 2
    d = np.maximum(dist, 1).astype(np.float32)
    large = max_exact + (np.log(d / max_exact) / math.log(MAX_DISTANCE / max_exact)
                         * (N_BUCKETS - max_exact)).astype(np.int32)
    large = np.minimum(large, N_BUCKETS - 1)
    return np.where(dist < max_exact, dist, large)


def _strip_distances(n_delta, t):
    return np.arange(n_delta)[:, None] * t + np.arange(2 * t)[None, :] - t


def _causal_log_mult(dist):
    return np.where(dist >= 0, 0.0, -np.inf)


def _dilated_log_mult(dist):
    mult = np.zeros(dist.shape, np.float64)
    for window, dil in DILATED_GROUPS:
        mult += (dist >= 0) & (dist <= window) & (dist % dil == 0)
    with np.errstate(divide="ignore"):
        return np.log(mult)


def _bias_strips(rel_bias, dist, log_mult):
    shown = np.isfinite(log_mult)
    bucket = _rel_bucket_np(dist)
    last_uniform = bool(shown[-1].all() and (log_mult[-1] == log_mult[-1, 0]).all()
                        and (bucket[-1] == bucket[-1, 0]).all())
    offset = jnp.asarray(np.where(shown, log_mult * LOG2E, NEG), F32)
    if rel_bias is None:
        return offset[None], last_uniform
    bias = jnp.take(rel_bias.astype(F32), jnp.asarray(bucket), axis=0)
    strips = jnp.where(jnp.asarray(shown)[..., None], bias * LOG2E + offset[..., None], NEG)
    return jnp.moveaxis(strips, -1, 0), last_uniform


def _attn_kernel(*refs, kind, g, t, seq, n_delta, last_uniform, lambda_init):
    n_in = {"dilated": 4, "moba": 4, "diff": 6, "fox": 6}[kind]
    q_ref, k_ref, v_ref, strip_ref = refs[:4]
    o_ref = refs[n_in]
    tab_ref, vt_ref, s_ref = refs[n_in + 1:n_in + 4]
    hg = pl.program_id(0)
    nk = seq // t
    n_tab = tab_ref.shape[0]
    n_maps = s_ref.shape[0] // (2 * g)
    dh = HEAD_DIM

    @pl.when(pl.program_id(1) == 0)
    def _():
        for hh in range(n_tab):
            for dl in range(n_delta):
                strip = jnp.broadcast_to(strip_ref[hh, dl:dl + 1, :], (t, 2 * t))
                tab_ref[hh, dl] = pltpu.roll(strip, 0, 1, stride=1, stride_axis=0)[:, t:]

    def tile(j):
        return slice(j * t, (j + 1) * t)

    def head_passes(hh):
        head = slice(hh * dh, (hh + 1) * dh)
        ones_row = lax.broadcasted_iota(jnp.int32, (DEN_ROWS, seq), 0) == 0
        vt_ref[hh, dh:, :] = jnp.where(ones_row, 1.0, 0.0).astype(BF16)
        for jb in range(nk):
            vt_ref[hh, :dh, tile(jb)] = v_ref[tile(jb), head].T
        if kind == "moba":
            kmean_ref = refs[n_in + 4]
            for jb in range(nk):
                kmean_ref[hh, jb:jb + 1, :] = jnp.mean(k_ref[tile(jb), head].astype(F32), axis=0, keepdims=True)
            km = kmean_ref[hh]
            km_hi = km.astype(BF16)
            km_lo = (km - km_hi.astype(F32)).astype(BF16)
        if kind == "fox":
            cs_ref, cst_ref, ckb_ref = refs[4], refs[5], refs[n_in + 4]
            cs = cs_ref[...]
            lane = lax.broadcasted_iota(jnp.int32, cs.shape, 1)
            ck = jnp.sum(jnp.where(lane == hg * g + hh, cs, 0.0), axis=-1, keepdims=True)
            ckb_ref[hh] = jnp.broadcast_to(ck, cs.shape)
        if kind == "diff":
            lam_ref, sub_ref = refs[4], refs[5]
            lam = lam_ref[...]
            lmbda = (jnp.exp(jnp.sum(lam[0:1, :] * lam[1:2, :], axis=-1, keepdims=True))
                     - jnp.exp(jnp.sum(lam[2:3, :] * lam[3:4, :], axis=-1, keepdims=True)) + lambda_init)
        far_term = strip_ref[hh % n_tab, n_delta - 1:n_delta, 0:1] if last_uniform else None

        def is_far(i, j):
            return last_uniform and i - j >= n_delta - 1

        def score_pass(i):
            qt = q_ref[tile(i), head].T
            if kind == "diff":
                row = lax.broadcasted_iota(jnp.int32, qt.shape, 0)
                qts = [jnp.where(row < DIFF_HEAD_DIM, qt, jnp.zeros_like(qt)),
                       jnp.where(row >= DIFF_HEAD_DIM, qt, jnp.zeros_like(qt))]
            else:
                qts = [qt]
            sel = None
            if kind == "moba" and i > 0:
                gate = (jnp.dot(km_hi, qt, preferred_element_type=F32)
                        + jnp.dot(km_lo, qt, preferred_element_type=F32))
                blk = lax.broadcasted_iota(jnp.int32, gate.shape, 0)
                sel = []
                for jb in range(i):
                    gj = gate[jb:jb + 1, :]
                    beats = (blk < i) & ((gate > gj) | ((gate == gj) & (blk < jb)))
                    rank = jnp.sum(jnp.where(beats, 1.0, 0.0), axis=0, keepdims=True)
                    sel.append(jnp.where(rank < MOBA_TOPK, 0.0, NEG))
            maxima = []
            for mp in range(n_maps):
                buf = (hh * 2 + i % 2) * n_maps + mp
                m = None
                for j in range(i + 1):
                    s = jnp.dot(k_ref[tile(j), head], qts[mp], preferred_element_type=F32)
                    if kind == "fox":
                        s = s + cst_ref[hh, i:i + 1, :] - jnp.concatenate([ckb_ref[hh, tile(j), :]] * (t // dh), axis=1)
                    if not is_far(i, j):
                        s = s + tab_ref[hh % n_tab, min(i - j, n_delta - 1)]
                    if kind == "moba" and j < i:
                        s = s + sel[j]
                    s_ref[buf, tile(j), :] = s
                    tile_max = jnp.max(s, axis=0, keepdims=True)
                    if is_far(i, j):
                        tile_max = tile_max + far_term
                    m = tile_max if m is None else jnp.maximum(m, tile_max)
                maxima.append(m)
            return maxima

        def value_pass(i, maxima):
            outs = []
            for mp in range(n_maps):
                buf = (hh * 2 + i % 2) * n_maps + mp
                acc = None
                for j in range(i + 1):
                    shift = maxima[mp] - far_term if is_far(i, j) else maxima[mp]
                    p = jnp.exp2(s_ref[buf, tile(j), :] - shift).astype(BF16)
                    part = jnp.dot(vt_ref[hh, :, tile(j)], p, preferred_element_type=F32)
                    acc = part if acc is None else acc + part
                outs.append(acc[:dh, :] / acc[dh:dh + 1, :])
            if kind == "diff":
                out = _rms_rows((outs[0] - lmbda * outs[1]).T, sub_ref[...]) * (1.0 - lambda_init)
            else:
                out = outs[0].T
            o_ref[tile(i), head] = out.astype(o_ref.dtype)

        return score_pass, value_pass

    passes = [head_passes(hh) for hh in range(g)]
    maxima = [score_pass(0) for score_pass, _ in passes]
    for i in range(nk):
        next_maxima = [score_pass(i + 1) for score_pass, _ in passes] if i + 1 < nk else None
        for (_, value_pass), head_maxima in zip(passes, maxima):
            value_pass(i, head_maxima)
        maxima = next_maxima


def attention(kind, qkv, strips, last_uniform, batch, seq, extra=(), lambda_init=0.0):
    t, g, dh = ATTN_TILE, ATTN_HEADS_PER_STEP, HEAD_DIM
    ng = N_HEADS // g
    m = qkv.shape[0]
    per_head = strips.shape[0] == N_HEADS
    n_tab = g if per_head else 1
    n_delta = strips.shape[1]
    n_maps = 2 if kind == "diff" else 1
    in_specs = [
        pl.BlockSpec((seq, g * dh), lambda h, b: (b, h)),
        pl.BlockSpec((seq, g * dh), lambda h, b: (b, ng + h)),
        pl.BlockSpec((seq, g * dh), lambda h, b: (b, 2 * ng + h)),
        pl.BlockSpec((n_tab, n_delta, 2 * t), (lambda h, b: (h, 0, 0)) if per_head else (lambda h, b: (0, 0, 0))),
    ]
    args = [qkv, qkv, qkv, strips]
    scratch = [
        pltpu.VMEM((n_tab, n_delta, t, t), F32),
        pltpu.VMEM((g, dh + DEN_ROWS, seq), BF16),
        pltpu.VMEM((g * 2 * n_maps, seq, t), F32),
    ]
    if kind == "moba":
        scratch.append(pltpu.VMEM((g, seq // t, dh), F32))
    elif kind == "diff":
        lam, subln = extra
        in_specs += [pl.BlockSpec(lam.shape, lambda h, b: (0, 0)),
                     pl.BlockSpec((1, dh), lambda h, b: (0, 0))]
        args += [lam, subln.reshape(1, dh)]
    elif kind == "fox":
        csum, csum_t = extra
        in_specs += [pl.BlockSpec((None, seq, dh), lambda h, b: (b, 0, 0)),
                     pl.BlockSpec((None, g, seq // t, t), lambda h, b: (b, h, 0, 0))]
        args += [csum, csum_t]
        scratch.append(pltpu.VMEM((g, seq, dh), F32))
    return pl.pallas_call(
        functools.partial(_attn_kernel, kind=kind, g=g, t=t, seq=seq, n_delta=n_delta,
                          last_uniform=last_uniform, lambda_init=lambda_init),
        grid=(ng, batch),
        in_specs=in_specs,
        out_specs=pl.BlockSpec((seq, g * dh), lambda h, b: (b, h)),
        out_shape=jax.ShapeDtypeStruct((m, N_HEADS * dh), BF16),
        scratch_shapes=scratch,
        compiler_params=_params(("parallel", "arbitrary")),
        name=kind + "_attention",
    )(*args)


def _fox_gate_kernel(f_ref, b_ref, cs_ref, cst_ref):
    z = f_ref[...] + b_ref[...]
    x = jnp.minimum(z, 0.0) - jnp.log1p(jnp.exp(-jnp.abs(z)))
    row = lax.broadcasted_iota(jnp.int32, x.shape, 0)
    shift = 1
    while shift < x.shape[0]:
        x = x + jnp.where(row >= shift, pltpu.roll(x, shift, 0), 0.0)
        shift *= 2
    x = x * LOG2E
    cs_ref[...] = x
    cst_ref[...] = x.T


def fox_gate(f, b_pad):
    batch, seq, lanes = f.shape
    return pl.pallas_call(
        _fox_gate_kernel,
        grid=(batch,),
        in_specs=[pl.BlockSpec((None, seq, lanes), lambda b: (b, 0, 0)),
                  pl.BlockSpec((1, lanes), lambda b: (0, 0))],
        out_specs=[pl.BlockSpec((None, seq, lanes), lambda b: (b, 0, 0)),
                   pl.BlockSpec((None, lanes, seq), lambda b: (b, 0, 0))],
        out_shape=[jax.ShapeDtypeStruct((batch, seq, lanes), F32),
                   jax.ShapeDtypeStruct((batch, lanes, seq), F32)],
        compiler_params=_params(("parallel",)),
        name="fox_gate_cumsum",
    )(f, b_pad)


def kernel(x, rel_bias, norm_gains, w_out, w_up, conv_w, conv_b, w_down, w_in_dil, w_in_moba,
           w_in_diff, lambda_diff, subln_diff, w_in_fox, b_fox):
    batch, seq, d = x.shape
    depth = norm_gains.shape[0]
    t = ATTN_TILE
    assert seq % t == 0 and t == MOBA_BLOCK and d == N_HEADS * HEAD_DIM
    causal3 = _strip_distances(3, t)
    causal_strips, causal_uniform = _bias_strips(rel_bias, causal3, _causal_log_mult(causal3))

    def qkv_scale(head_scale):
        return jnp.concatenate([jnp.full((d,), head_scale * LOG2E, F32), jnp.ones((2 * d,), F32)])

    w_out_b, w_down_b = cast_bf16(w_out), cast_bf16(w_down)

    h = x.reshape(batch * seq, d)
    u = rms_norm_bf16(h, norm_gains[0, 0])
    for i in range(depth):
        mixer, j = i % N_MIXERS, i // N_MIXERS
        g = norm_gains[i]
        if mixer == 0:
            qkv = proj_matmul(u, w_in_dil, j, qkv_scale(HEAD_DIM ** -0.5))
            dist = _strip_distances(4, t)
            strips, uniform = _bias_strips(rel_bias, dist, _dilated_log_mult(dist))
            y = attention("dilated", qkv, strips, uniform, batch, seq)
        elif mixer == 1:
            qkv = proj_matmul(u, w_in_moba, j, qkv_scale(HEAD_DIM ** -0.5))
            y = attention("moba", qkv, causal_strips, causal_uniform, batch, seq)
        elif mixer == 2:
            qkv = proj_matmul(u, w_in_diff, j, qkv_scale(DIFF_HEAD_DIM ** -0.5))
            y = attention("diff", qkv, causal_strips, causal_uniform, batch, seq,
                          extra=(lambda_diff[j], subln_diff[j]), lambda_init=0.8 - 0.6 * math.exp(-0.3 * i))
        else:
            w_fox_t = jnp.swapaxes(w_in_fox, 1, 2)
            qkv = proj_matmul(u, w_fox_t, j, qkv_scale(HEAD_DIM ** -0.5), w_transposed=True)
            w_f = jnp.pad(w_fox_t[j, 3 * d:, :], ((0, HEAD_DIM - N_HEADS), (0, 0)))[None]
            f = proj_matmul(u, w_f, 0, jnp.ones((HEAD_DIM,), F32), w_transposed=True, out_dtype=F32)
            b_pad = jnp.pad(b_fox[j], (0, HEAD_DIM - N_HEADS)).reshape(1, HEAD_DIM)
            csum, csum_t = fox_gate(f.reshape(batch, seq, HEAD_DIM), b_pad)
            dist = _strip_distances(2, t)
            strips, uniform = _bias_strips(None, dist, _causal_log_mult(dist))
            y = attention("fox", qkv, strips, uniform, batch, seq,
                          extra=(csum, csum_t.reshape(batch, HEAD_DIM, seq // t, t)))
        h, u = matmul_norm_res(y, w_out_b, i, g[1], h, g[2], tk=d)
        hidden = ffn_up(u, w_up, i, conv_w[i], conv_b[i], seq)
        h, u = matmul_norm_res(hidden, w_down_b, i, g[3], h, norm_gains[i + 1, 0] if i + 1 < depth else None,
                               tm=1024, tk=1024, single_buffer_res=True)
    return h.reshape(batch, seq, d)
```

```python
import functools
import math

import jax
import jax.numpy as jnp
import numpy as np
from jax import lax
from jax.experimental import pallas as pl
from jax.experimental.pallas import tpu as pltpu

N_HEADS = 16
HEAD_DIM = 128
N_MIXERS = 4
CONV_WIDTH = 3
RMS_EPS = 1e-6
N_BUCKETS = 32
MAX_DISTANCE = 128
DILATED_GROUPS = ((128, 1), (512, 4), (2048, 16))
MOBA_BLOCK = 256
MOBA_TOPK = 3
DIFF_HEAD_DIM = HEAD_DIM // 2
NEG = -1e30
LOG2E = math.log2(math.e)

ATTN_TILE = 256
ATTN_HEADS_PER_STEP = 2
DEN_ROWS = 16
CONV_HALO = 16
FINISH_ROW_CHUNK = 128
CAST_BLOCK_ELEMS = 2 * 1024 * 1024
VMEM_LIMIT = 52 * 1024 * 1024

BF16 = jnp.bfloat16
F32 = jnp.float32


def _rms_rows(x, g):
    return x * lax.rsqrt(jnp.mean(x * x, axis=-1, keepdims=True) + RMS_EPS) * g


def _params(sem, vmem=VMEM_LIMIT):
    return pltpu.CompilerParams(dimension_semantics=sem, vmem_limit_bytes=vmem)


def _cast_kernel(x_ref, o_ref):
    o_ref[...] = x_ref[...].astype(o_ref.dtype)


def cast_bf16(w, layers):
    _, k, n = w.shape
    rows = 1 << (min(k, CAST_BLOCK_ELEMS // n).bit_length() - 1)
    assert k % rows == 0
    spec = pl.BlockSpec((None, rows, n), lambda l, r: (l, r, 0))
    return pl.pallas_call(
        _cast_kernel,
        grid=(layers, k // rows),
        in_specs=[spec],
        out_specs=spec,
        out_shape=jax.ShapeDtypeStruct((layers, k, n), BF16),
        compiler_params=_params(("parallel", "parallel")),
        name="cast_bf16",
    )(w)


def _rms_norm_kernel(x_ref, g_ref, o_ref):
    o_ref[...] = _rms_rows(x_ref[...], g_ref[...]).astype(o_ref.dtype)


def rms_norm_bf16(x, g, *, tm=512):
    m, d = x.shape
    return pl.pallas_call(
        _rms_norm_kernel,
        grid=(m // tm,),
        in_specs=[pl.BlockSpec((tm, d), lambda i: (i, 0)), pl.BlockSpec((1, d), lambda i: (0, 0))],
        out_specs=pl.BlockSpec((tm, d), lambda i: (i, 0)),
        out_shape=jax.ShapeDtypeStruct((m, d), BF16),
        compiler_params=_params(("parallel",)),
        name="rms_norm_bf16",
    )(x, g.reshape(1, d))


def _proj_kernel(u_ref, w_ref, cs_ref, o_ref, wb_ref, *, w_transposed):
    @pl.when(pl.program_id(1) == 0)
    def _():
        wb = w_ref[...].astype(BF16)
        wb_ref[...] = wb.T if w_transposed else wb

    acc = jnp.dot(u_ref[...], wb_ref[...], preferred_element_type=F32)
    o_ref[...] = (acc * cs_ref[...]).astype(o_ref.dtype)


def proj_matmul(u, w, layer, col_scale, *, w_transposed=False, out_dtype=BF16, tm=1024, tn=1024):
    m, d = u.shape
    n = col_scale.shape[0]
    tn = min(tn, n)
    w_spec = (pl.BlockSpec((None, tn, d), lambda j, i: (layer, j, 0)) if w_transposed
              else pl.BlockSpec((None, d, tn), lambda j, i: (layer, 0, j)))
    return pl.pallas_call(
        functools.partial(_proj_kernel, w_transposed=w_transposed),
        grid=(n // tn, m // tm),
        in_specs=[
            pl.BlockSpec((tm, d), lambda j, i: (i, 0)),
            w_spec,
            pl.BlockSpec((1, tn), lambda j, i: (0, j)),
        ],
        out_specs=pl.BlockSpec((tm, tn), lambda j, i: (i, j)),
        out_shape=jax.ShapeDtypeStruct((m, n), out_dtype),
        scratch_shapes=[pltpu.VMEM((d, tn), BF16)],
        compiler_params=_params(("parallel", "arbitrary")),
        name="proj_matmul",
    )(u, w, col_scale.reshape(1, n))


def _matmul_norm_res_kernel(y_ref, w_ref, g_ref, res_ref, *rest, nk, has_next):
    if has_next:
        gn_ref, o_ref, u_ref = rest
    else:
        (o_ref,) = rest
    k = pl.program_id(1)

    def last_step(first):
        for r in range(0, o_ref.shape[0], FINISH_ROW_CHUNK):
            rows = slice(r, r + FINISH_ROW_CHUNK)
            acc = jnp.dot(y_ref[rows, :], w_ref[...], preferred_element_type=F32)
            if not first:
                acc = acc + o_ref[rows, :]
            h = res_ref[rows, :] + _rms_rows(acc, g_ref[...])
            o_ref[rows, :] = h
            if has_next:
                u_ref[rows, :] = _rms_rows(h, gn_ref[...]).astype(u_ref.dtype)

    if nk == 1:
        last_step(True)
        return

    @pl.when(k == 0)
    def _():
        o_ref[...] = jnp.dot(y_ref[...], w_ref[...], preferred_element_type=F32)

    @pl.when((k > 0) & (k < nk - 1))
    def _():
        o_ref[...] += jnp.dot(y_ref[...], w_ref[...], preferred_element_type=F32)

    @pl.when(k == nk - 1)
    def _():
        last_step(False)


def matmul_norm_res(y, w, layer, g, res, g_next=None, *, tm=512, tk=2048):
    m, kdim = y.shape
    d = w.shape[2]
    nk = kdim // tk
    has_next = g_next is not None
    row_spec = pl.BlockSpec((tm, d), lambda i, k: (i, 0))
    gain_spec = pl.BlockSpec((1, d), lambda i, k: (0, 0))
    in_specs = [
        pl.BlockSpec((tm, tk), lambda i, k: (i, k)),
        pl.BlockSpec((None, tk, d), lambda i, k: (layer, k, 0)),
        gain_spec,
        row_spec,
    ]
    args = [y, w, g.reshape(1, d), res]
    out_shape = [jax.ShapeDtypeStruct((m, d), F32)]
    out_specs = [row_spec]
    if has_next:
        in_specs.append(gain_spec)
        args.append(g_next.reshape(1, d))
        out_shape.append(jax.ShapeDtypeStruct((m, d), BF16))
        out_specs.append(row_spec)
    outs = pl.pallas_call(
        functools.partial(_matmul_norm_res_kernel, nk=nk, has_next=has_next),
        grid=(m // tm, nk),
        in_specs=in_specs,
        out_specs=out_specs,
        out_shape=out_shape,
        compiler_params=_params(("parallel", "arbitrary")),
        name="matmul_norm_res",
    )(*args)
    return outs if has_next else (outs[0], None)


def _gelu_tanh(x):
    return x * (0.5 * (1.0 + jnp.tanh(math.sqrt(2.0 / math.pi) * (x + 0.044715 * (x * x * x)))))


def _causal_conv3(a, tail, cw, cb):
    a_ext = jnp.concatenate([tail, a], axis=0)
    r1 = pltpu.roll(a_ext, 1, 0)
    r2 = pltpu.roll(a_ext, 2, 0)
    y = cw[0:1, :] * r2 + cw[1:2, :] * r1 + cw[2:3, :] * a_ext
    return y[CONV_HALO:, :] + cb


def _ffn_up_kernel(u_ref, wg_ref, wu_ref, cwg_ref, cwu_ref, cbg_ref, cbu_ref, *rest, tm, seq, n_side):
    side_in, rest = rest[:n_side], rest[n_side:]
    o_ref, side_out = rest[0], rest[1:1 + n_side]
    wgb_ref, wub_ref, carry_ref = rest[1 + n_side:]
    i = pl.program_id(1)

    for src_ref, dst_ref in zip(side_in, side_out):
        dst_ref[...] = src_ref[...].astype(dst_ref.dtype)

    @pl.when(i == 0)
    def _():
        wgb_ref[...] = wg_ref[...].astype(BF16)
        wub_ref[...] = wu_ref[...].astype(BF16)

    @pl.when((i * tm) % seq == 0)
    def _():
        carry_ref[...] = jnp.zeros(carry_ref.shape, F32)

    u = u_ref[...]
    act = []
    for idx, (wb_ref, cw_ref, cb_ref) in enumerate(((wgb_ref, cwg_ref, cbg_ref), (wub_ref, cwu_ref, cbu_ref))):
        a = jnp.dot(u, wb_ref[...], preferred_element_type=F32)
        act.append(_causal_conv3(a, carry_ref[idx], cw_ref[...], cb_ref[...]))
        carry_ref[idx] = a[tm - CONV_HALO:, :]
    o_ref[...] = (_gelu_tanh(act[0]) * act[1]).astype(o_ref.dtype)


def ffn_up(u, w_up, layer, conv_w, conv_b, seq, side_casts=(), *, tm=1024, tn=512):
    m, d = u.shape
    f = w_up.shape[2] // 2
    nn = f // tn
    nm = m // tm
    cb = conv_b.reshape(1, 2 * f)
    side_specs, side_shapes = [], []
    for w, l in side_casts:
        rows = w.shape[1] // (nn * nm)
        assert rows * nn * nm == w.shape[1] and rows % 16 == 0
        side_specs.append(pl.BlockSpec((None, rows, w.shape[2]), lambda j, i, l=l: (l, j * nm + i, 0)))
        side_shapes.append(jax.ShapeDtypeStruct((1,) + w.shape[1:], BF16))
    side_out_specs = [pl.BlockSpec(s.block_shape, lambda j, i: (0, j * nm + i, 0)) for s in side_specs]
    outs = pl.pallas_call(
        functools.partial(_ffn_up_kernel, tm=tm, seq=seq, n_side=len(side_casts)),
        grid=(nn, nm),
        in_specs=[
            pl.BlockSpec((tm, d), lambda j, i: (i, 0)),
            pl.BlockSpec((None, d, tn), lambda j, i: (layer, 0, j)),
            pl.BlockSpec((None, d, tn), lambda j, i: (layer, 0, nn + j)),
            pl.BlockSpec((CONV_WIDTH, tn), lambda j, i: (0, j)),
            pl.BlockSpec((CONV_WIDTH, tn), lambda j, i: (0, nn + j)),
            pl.BlockSpec((1, tn), lambda j, i: (0, j)),
            pl.BlockSpec((1, tn), lambda j, i: (0, nn + j)),
        ] + side_specs,
        out_specs=[pl.BlockSpec((tm, tn), lambda j, i: (i, j))] + side_out_specs,
        out_shape=[jax.ShapeDtypeStruct((m, f), BF16)] + side_shapes,
        scratch_shapes=[pltpu.VMEM((d, tn), BF16), pltpu.VMEM((d, tn), BF16),
                        pltpu.VMEM((2, CONV_HALO, tn), F32)],
        compiler_params=_params(("arbitrary", "arbitrary")),
        name="ffn_up_conv_glu",
    )(u, w_up, w_up, conv_w, conv_w, cb, cb, *[w for w, _ in side_casts])
    return outs


def _rel_bucket_np(dist):
    dist = np.maximum(dist, 0)
    max_exact = N_BUCKETS // 2
    d = np.maximum(dist, 1).astype(np.float32)
    large = max_exact + (np.log(d / max_exact) / math.log(MAX_DISTANCE / max_exact)
                         * (N_BUCKETS - max_exact)).astype(np.int32)
    large = np.minimum(large, N_BUCKETS - 1)
    return np.where(dist < max_exact, dist, large)


def _strip_distances(n_delta, t):
    return np.arange(n_delta)[:, None] * t + np.arange(2 * t)[None, :] - t


def _causal_log_mult(dist):
    return np.where(dist >= 0, 0.0, -np.inf)


def _dilated_log_mult(dist):
    mult = np.zeros(dist.shape, np.float64)
    for window, dil in DILATED_GROUPS:
        mult += (dist >= 0) & (dist <= window) & (dist % dil == 0)
    with np.errstate(divide="ignore"):
        return np.log(mult)


def _bias_strips(rel_bias, dist, log_mult):
    shown = np.isfinite(log_mult)
    bucket = _rel_bucket_np(dist)
    last_uniform = bool(shown[-1].all() and (log_mult[-1] == log_mult[-1, 0]).all()
                        and (bucket[-1] == bucket[-1, 0]).all())
    offset = jnp.asarray(np.where(shown, log_mult * LOG2E, NEG), F32)
    if rel_bias is None:
        return offset[None], last_uniform
    bias = jnp.take(rel_bias.astype(F32), jnp.asarray(bucket), axis=0)
    strips = jnp.where(jnp.asarray(shown)[..., None], bias * LOG2E + offset[..., None], NEG)
    return jnp.moveaxis(strips, -1, 0), last_uniform


def _attn_kernel(*refs, kind, g, t, seq, n_delta, last_uniform, lambda_init):
    n_in = {"dilated": 4, "moba": 4, "diff": 6, "fox": 6}[kind]
    q_ref, k_ref, v_ref, strip_ref = refs[:4]
    o_ref = refs[n_in]
    tab_ref, vt_ref, s_ref = refs[n_in + 1:n_in + 4]
    hg = pl.program_id(0)
    nk = seq // t
    n_tab = tab_ref.shape[0]
    n_maps = s_ref.shape[0] // (2 * g)
    dh = HEAD_DIM

    @pl.when(pl.program_id(1) == 0)
    def _():
        for hh in range(n_tab):
            for dl in range(n_delta):
                strip = jnp.broadcast_to(strip_ref[hh, dl:dl + 1, :], (t, 2 * t))
                tab_ref[hh, dl] = pltpu.roll(strip, 0, 1, stride=1, stride_axis=0)[:, t:]

    def tile(j):
        return slice(j * t, (j + 1) * t)

    def head_passes(hh):
        head = slice(hh * dh, (hh + 1) * dh)
        ones_row = lax.broadcasted_iota(jnp.int32, (DEN_ROWS, seq), 0) == 0
        vt_ref[hh, dh:, :] = jnp.where(ones_row, 1.0, 0.0).astype(BF16)
        for jb in range(nk):
            vt_ref[hh, :dh, tile(jb)] = v_ref[tile(jb), head].T
        if kind == "moba":
            kmean_ref = refs[n_in + 4]
            for jb in range(nk):
                kmean_ref[hh, jb:jb + 1, :] = jnp.mean(k_ref[tile(jb), head].astype(F32), axis=0, keepdims=True)
            km = kmean_ref[hh]
            km_hi = km.astype(BF16)
            km_lo = (km - km_hi.astype(F32)).astype(BF16)
        if kind == "fox":
            cs_ref, cst_ref, ckb_ref = refs[4], refs[5], refs[n_in + 4]
            cs = cs_ref[...]
            lane = lax.broadcasted_iota(jnp.int32, cs.shape, 1)
            ck = jnp.sum(jnp.where(lane == hg * g + hh, cs, 0.0), axis=-1, keepdims=True)
            ckb_ref[hh] = jnp.broadcast_to(ck, cs.shape)
        if kind == "diff":
            lam_ref, sub_ref = refs[4], refs[5]
            lam = lam_ref[...]
            lmbda = (jnp.exp(jnp.sum(lam[0:1, :] * lam[1:2, :], axis=-1, keepdims=True))
                     - jnp.exp(jnp.sum(lam[2:3, :] * lam[3:4, :], axis=-1, keepdims=True)) + lambda_init)
        far_term = strip_ref[hh % n_tab, n_delta - 1:n_delta, 0:1] if last_uniform else None

        def is_far(i, j):
            return last_uniform and i - j >= n_delta - 1

        def score_pass(i):
            qt = q_ref[tile(i), head].T
            if kind == "diff":
                row = lax.broadcasted_iota(jnp.int32, qt.shape, 0)
                qts = [jnp.where(row < DIFF_HEAD_DIM, qt, jnp.zeros_like(qt)),
                       jnp.where(row >= DIFF_HEAD_DIM, qt, jnp.zeros_like(qt))]
            else:
                qts = [qt]
            sel = None
            if kind == "moba" and i > 0:
                gate = (jnp.dot(km_hi, qt, preferred_element_type=F32)
                        + jnp.dot(km_lo, qt, preferred_element_type=F32))
                blk = lax.broadcasted_iota(jnp.int32, gate.shape, 0)
                sel = []
                for jb in range(i):
                    gj = gate[jb:jb + 1, :]
                    beats = (blk < i) & ((gate > gj) | ((gate == gj) & (blk < jb)))
                    rank = jnp.sum(jnp.where(beats, 1.0, 0.0), axis=0, keepdims=True)
                    sel.append(jnp.where(rank < MOBA_TOPK, 0.0, NEG))
            maxima = []
            for mp in range(n_maps):
                buf = (hh * 2 + i % 2) * n_maps + mp
                m = None
                for j in range(i + 1):
                    s = jnp.dot(k_ref[tile(j), head], qts[mp], preferred_element_type=F32)
                    if kind == "fox":
                        s = s + cst_ref[hh, i:i + 1, :] - jnp.concatenate([ckb_ref[hh, tile(j), :]] * (t // dh), axis=1)
                    if not is_far(i, j):
                        s = s + tab_ref[hh % n_tab, min(i - j, n_delta - 1)]
                    if kind == "moba" and j < i:
                        s = s + sel[j]
                    s_ref[buf, tile(j), :] = s
                    tile_max = jnp.max(s, axis=0, keepdims=True)
                    if is_far(i, j):
                        tile_max = tile_max + far_term
                    m = tile_max if m is None else jnp.maximum(m, tile_max)
                maxima.append(m)
            return maxima

        def value_pass(i, maxima):
            outs = []
            for mp in range(n_maps):
                buf = (hh * 2 + i % 2) * n_maps + mp
                acc = None
                for j in range(i + 1):
                    shift = maxima[mp] - far_term if is_far(i, j) else maxima[mp]
                    p = jnp.exp2(s_ref[buf, tile(j), :] - shift).astype(BF16)
                    part = jnp.dot(vt_ref[hh, :, tile(j)], p, preferred_element_type=F32)
                    acc = part if acc is None else acc + part
                outs.append(acc[:dh, :] / acc[dh:dh + 1, :])
            if kind == "diff":
                out = _rms_rows((outs[0] - lmbda * outs[1]).T, sub_ref[...]) * (1.0 - lambda_init)
            else:
                out = outs[0].T
            o_ref[tile(i), head] = out.astype(o_ref.dtype)

        return score_pass, value_pass

    passes = [head_passes(hh) for hh in range(g)]
    maxima = [score_pass(0) for score_pass, _ in passes]
    for i in range(nk):
        next_maxima = [score_pass(i + 1) for score_pass, _ in passes] if i + 1 < nk else None
        for (_, value_pass), head_maxima in zip(passes, maxima):
            value_pass(i, head_maxima)
        maxima = next_maxima


def attention(kind, qkv, strips, last_uniform, batch, seq, extra=(), lambda_init=0.0):
    t, g, dh = ATTN_TILE, ATTN_HEADS_PER_STEP, HEAD_DIM
    ng = N_HEADS // g
    m = qkv.shape[0]
    per_head = strips.shape[0] == N_HEADS
    n_tab = g if per_head else 1
    n_delta = strips.shape[1]
    n_maps = 2 if kind == "diff" else 1
    in_specs = [
        pl.BlockSpec((seq, g * dh), lambda h, b: (b, h)),
        pl.BlockSpec((seq, g * dh), lambda h, b: (b, ng + h)),
        pl.BlockSpec((seq, g * dh), lambda h, b: (b, 2 * ng + h)),
        pl.BlockSpec((n_tab, n_delta, 2 * t), (lambda h, b: (h, 0, 0)) if per_head else (lambda h, b: (0, 0, 0))),
    ]
    args = [qkv, qkv, qkv, strips]
    scratch = [
        pltpu.VMEM((n_tab, n_delta, t, t), F32),
        pltpu.VMEM((g, dh + DEN_ROWS, seq), BF16),
        pltpu.VMEM((g * 2 * n_maps, seq, t), F32),
    ]
    if kind == "moba":
        scratch.append(pltpu.VMEM((g, seq // t, dh), F32))
    elif kind == "diff":
        lam, subln = extra
        in_specs += [pl.BlockSpec(lam.shape, lambda h, b: (0, 0)),
                     pl.BlockSpec((1, dh), lambda h, b: (0, 0))]
        args += [lam, subln.reshape(1, dh)]
    elif kind == "fox":
        csum, csum_t = extra
        in_specs += [pl.BlockSpec((None, seq, dh), lambda h, b: (b, 0, 0)),
                     pl.BlockSpec((None, g, seq // t, t), lambda h, b: (b, h, 0, 0))]
        args += [csum, csum_t]
        scratch.append(pltpu.VMEM((g, seq, dh), F32))
    return pl.pallas_call(
        functools.partial(_attn_kernel, kind=kind, g=g, t=t, seq=seq, n_delta=n_delta,
                          last_uniform=last_uniform, lambda_init=lambda_init),
        grid=(ng, batch),
        in_specs=in_specs,
        out_specs=pl.BlockSpec((seq, g * dh), lambda h, b: (b, h)),
        out_shape=jax.ShapeDtypeStruct((m, N_HEADS * dh), BF16),
        scratch_shapes=scratch,
        compiler_params=_params(("parallel", "arbitrary")),
        name=kind + "_attention",
    )(*args)


def _fox_gate_kernel(f_ref, b_ref, cs_ref, cst_ref):
    z = f_ref[...] + b_ref[...]
    x = jnp.minimum(z, 0.0) - jnp.log1p(jnp.exp(-jnp.abs(z)))
    row = lax.broadcasted_iota(jnp.int32, x.shape, 0)
    shift = 1
    while shift < x.shape[0]:
        x = x + jnp.where(row >= shift, pltpu.roll(x, shift, 0), 0.0)
        shift *= 2
    x = x * LOG2E
    cs_ref[...] = x
    cst_ref[...] = x.T


def fox_gate(f, b_pad):
    batch, seq, lanes = f.shape
    return pl.pallas_call(
        _fox_gate_kernel,
        grid=(batch,),
        in_specs=[pl.BlockSpec((None, seq, lanes), lambda b: (b, 0, 0)),
                  pl.BlockSpec((1, lanes), lambda b: (0, 0))],
        out_specs=[pl.BlockSpec((None, seq, lanes), lambda b: (b, 0, 0)),
                   pl.BlockSpec((None, lanes, seq), lambda b: (b, 0, 0))],
        out_shape=[jax.ShapeDtypeStruct((batch, seq, lanes), F32),
                   jax.ShapeDtypeStruct((batch, lanes, seq), F32)],
        compiler_params=_params(("parallel",)),
        name="fox_gate_cumsum",
    )(f, b_pad)


def kernel(x, rel_bias, norm_gains, w_out, w_up, conv_w, conv_b, w_down, w_in_dil, w_in_moba,
           w_in_diff, lambda_diff, subln_diff, w_in_fox, b_fox):
    batch, seq, d = x.shape
    depth = norm_gains.shape[0]
    t = ATTN_TILE
    assert seq % t == 0 and t == MOBA_BLOCK and d == N_HEADS * HEAD_DIM
    causal3 = _strip_distances(3, t)
    causal_strips, causal_uniform = _bias_strips(rel_bias, causal3, _causal_log_mult(causal3))

    def qkv_scale(head_scale):
        return jnp.concatenate([jnp.full((d,), head_scale * LOG2E, F32), jnp.ones((2 * d,), F32)])

    w_out_b = cast_bf16(w_out, 1)

    h = x.reshape(batch * seq, d)
    u = rms_norm_bf16(h, norm_gains[0, 0])
    for i in range(depth):
        mixer, j = i % N_MIXERS, i // N_MIXERS
        g = norm_gains[i]
        if mixer == 0:
            qkv = proj_matmul(u, w_in_dil, j, qkv_scale(HEAD_DIM ** -0.5))
            dist = _strip_distances(4, t)
            strips, uniform = _bias_strips(rel_bias, dist, _dilated_log_mult(dist))
            y = attention("dilated", qkv, strips, uniform, batch, seq)
        elif mixer == 1:
            qkv = proj_matmul(u, w_in_moba, j, qkv_scale(HEAD_DIM ** -0.5))
            y = attention("moba", qkv, causal_strips, causal_uniform, batch, seq)
        elif mixer == 2:
            qkv = proj_matmul(u, w_in_diff, j, qkv_scale(DIFF_HEAD_DIM ** -0.5))
            y = attention("diff", qkv, causal_strips, causal_uniform, batch, seq,
                          extra=(lambda_diff[j], subln_diff[j]), lambda_init=0.8 - 0.6 * math.exp(-0.3 * i))
        else:
            w_fox_t = jnp.swapaxes(w_in_fox, 1, 2)
            qkv = proj_matmul(u, w_fox_t, j, qkv_scale(HEAD_DIM ** -0.5), w_transposed=True)
            w_f = jnp.pad(w_fox_t[j, 3 * d:, :], ((0, HEAD_DIM - N_HEADS), (0, 0)))[None]
            f = proj_matmul(u, w_f, 0, jnp.ones((HEAD_DIM,), F32), w_transposed=True, out_dtype=F32)
            b_pad = jnp.pad(b_fox[j], (0, HEAD_DIM - N_HEADS)).reshape(1, HEAD_DIM)
            csum, csum_t = fox_gate(f.reshape(batch, seq, HEAD_DIM), b_pad)
            dist = _strip_distances(2, t)
            strips, uniform = _bias_strips(None, dist, _causal_log_mult(dist))
            y = attention("fox", qkv, strips, uniform, batch, seq,
                          extra=(csum, csum_t.reshape(batch, HEAD_DIM, seq // t, t)))
        h, u = matmul_norm_res(y, w_out_b, 0, g[1], h, g[2], tk=d)
        side = [(w_down, i)] + ([(w_out, i + 1)] if i + 1 < depth else [])
        hidden, w_down_b, *w_out_next = ffn_up(u, w_up, i, conv_w[i], conv_b[i], seq, side)
        if w_out_next:
            w_out_b = w_out_next[0]
        h, u = matmul_norm_res(hidden, w_down_b, 0, g[3], h, norm_gains[i + 1, 0] if i + 1 < depth else None)
    return h.reshape(batch, seq, d)
```

```python
import functools
import math

import jax
import jax.numpy as jnp
import numpy as np
from jax import lax
from jax.experimental import pallas as pl
from jax.experimental.pallas import tpu as pltpu

N_HEADS = 16
HEAD_DIM = 128
N_MIXERS = 4
CONV_WIDTH = 3
RMS_EPS = 1e-6
N_BUCKETS = 32
MAX_DISTANCE = 128
DILATED_GROUPS = ((128, 1), (512, 4), (2048, 16))
MOBA_BLOCK = 256
MOBA_TOPK = 3
DIFF_HEAD_DIM = HEAD_DIM // 2
NEG = -1e30
LOG2E = math.log2(math.e)

ATTN_TILE = 256
ATTN_STREAMS_PER_STEP = 4
DEN_ROWS = 16
CONV_HALO = 16
FINISH_ROW_CHUNK = 128
CAST_BLOCK_ELEMS = 2 * 1024 * 1024
VMEM_LIMIT = 52 * 1024 * 1024

BF16 = jnp.bfloat16
F32 = jnp.float32


def _rms_rows(x, g):
    return x * lax.rsqrt(jnp.mean(x * x, axis=-1, keepdims=True) + RMS_EPS) * g


def _params(sem, vmem=VMEM_LIMIT):
    return pltpu.CompilerParams(dimension_semantics=sem, vmem_limit_bytes=vmem)


def _cast_kernel(x_ref, o_ref):
    o_ref[...] = x_ref[...].astype(o_ref.dtype)


def cast_bf16(w, layers):
    _, k, n = w.shape
    rows = 1 << (min(k, CAST_BLOCK_ELEMS // n).bit_length() - 1)
    assert k % rows == 0
    spec = pl.BlockSpec((None, rows, n), lambda l, r: (l, r, 0))
    return pl.pallas_call(
        _cast_kernel,
        grid=(layers, k // rows),
        in_specs=[spec],
        out_specs=spec,
        out_shape=jax.ShapeDtypeStruct((layers, k, n), BF16),
        compiler_params=_params(("parallel", "parallel")),
        name="cast_bf16",
    )(w)


def _rms_norm_kernel(x_ref, g_ref, o_ref):
    o_ref[...] = _rms_rows(x_ref[...], g_ref[...]).astype(o_ref.dtype)


def rms_norm_bf16(x, g, *, tm=512):
    m, d = x.shape
    return pl.pallas_call(
        _rms_norm_kernel,
        grid=(m // tm,),
        in_specs=[pl.BlockSpec((tm, d), lambda i: (i, 0)), pl.BlockSpec((1, d), lambda i: (0, 0))],
        out_specs=pl.BlockSpec((tm, d), lambda i: (i, 0)),
        out_shape=jax.ShapeDtypeStruct((m, d), BF16),
        compiler_params=_params(("parallel",)),
        name="rms_norm_bf16",
    )(x, g.reshape(1, d))


def _proj_kernel(u_ref, w_ref, cs_ref, o_ref, wb_ref, *, w_transposed):
    @pl.when(pl.program_id(1) == 0)
    def _():
        wb = w_ref[...].astype(BF16)
        wb_ref[...] = wb.T if w_transposed else wb

    acc = jnp.dot(u_ref[...], wb_ref[...], preferred_element_type=F32)
    o_ref[...] = (acc * cs_ref[...]).astype(o_ref.dtype)


def proj_matmul(u, w, layer, col_scale, *, w_transposed=False, out_dtype=BF16, tm=1024, tn=1024):
    m, d = u.shape
    n = col_scale.shape[0]
    tn = min(tn, n)
    w_spec = (pl.BlockSpec((None, tn, d), lambda j, i: (layer, j, 0)) if w_transposed
              else pl.BlockSpec((None, d, tn), lambda j, i: (layer, 0, j)))
    return pl.pallas_call(
        functools.partial(_proj_kernel, w_transposed=w_transposed),
        grid=(n // tn, m // tm),
        in_specs=[
            pl.BlockSpec((tm, d), lambda j, i: (i, 0)),
            w_spec,
            pl.BlockSpec((1, tn), lambda j, i: (0, j)),
        ],
        out_specs=pl.BlockSpec((tm, tn), lambda j, i: (i, j)),
        out_shape=jax.ShapeDtypeStruct((m, n), out_dtype),
        scratch_shapes=[pltpu.VMEM((d, tn), BF16)],
        compiler_params=_params(("parallel", "arbitrary")),
        name="proj_matmul",
    )(u, w, col_scale.reshape(1, n))


def _matmul_norm_res_kernel(y_ref, w_ref, g_ref, res_ref, *rest, nk, has_next):
    if has_next:
        gn_ref, o_ref, u_ref = rest
    else:
        (o_ref,) = rest
    k = pl.program_id(1)

    def last_step(first):
        for r in range(0, o_ref.shape[0], FINISH_ROW_CHUNK):
            rows = slice(r, r + FINISH_ROW_CHUNK)
            acc = jnp.dot(y_ref[rows, :], w_ref[...], preferred_element_type=F32)
            if not first:
                acc = acc + o_ref[rows, :]
            h = res_ref[rows, :] + _rms_rows(acc, g_ref[...])
            o_ref[rows, :] = h
            if has_next:
                u_ref[rows, :] = _rms_rows(h, gn_ref[...]).astype(u_ref.dtype)

    if nk == 1:
        last_step(True)
        return

    @pl.when(k == 0)
    def _():
        o_ref[...] = jnp.dot(y_ref[...], w_ref[...], preferred_element_type=F32)

    @pl.when((k > 0) & (k < nk - 1))
    def _():
        o_ref[...] += jnp.dot(y_ref[...], w_ref[...], preferred_element_type=F32)

    @pl.when(k == nk - 1)
    def _():
        last_step(False)


def matmul_norm_res(y, w, layer, g, res, g_next=None, *, tm=512, tk=2048):
    m, kdim = y.shape
    d = w.shape[2]
    nk = kdim // tk
    has_next = g_next is not None
    row_spec = pl.BlockSpec((tm, d), lambda i, k: (i, 0))
    gain_spec = pl.BlockSpec((1, d), lambda i, k: (0, 0))
    in_specs = [
        pl.BlockSpec((tm, tk), lambda i, k: (i, k)),
        pl.BlockSpec((None, tk, d), lambda i, k: (layer, k, 0)),
        gain_spec,
        row_spec,
    ]
    args = [y, w, g.reshape(1, d), res]
    out_shape = [jax.ShapeDtypeStruct((m, d), F32)]
    out_specs = [row_spec]
    if has_next:
        in_specs.append(gain_spec)
        args.append(g_next.reshape(1, d))
        out_shape.append(jax.ShapeDtypeStruct((m, d), BF16))
        out_specs.append(row_spec)
    outs = pl.pallas_call(
        functools.partial(_matmul_norm_res_kernel, nk=nk, has_next=has_next),
        grid=(m // tm, nk),
        in_specs=in_specs,
        out_specs=out_specs,
        out_shape=out_shape,
        compiler_params=_params(("parallel", "arbitrary")),
        name="matmul_norm_res",
    )(*args)
    return outs if has_next else (outs[0], None)


def _gelu_tanh(x):
    return x * (0.5 * (1.0 + jnp.tanh(math.sqrt(2.0 / math.pi) * (x + 0.044715 * (x * x * x)))))


def _causal_conv3(a, tail, cw, cb):
    a_ext = jnp.concatenate([tail, a], axis=0)
    r1 = pltpu.roll(a_ext, 1, 0)
    r2 = pltpu.roll(a_ext, 2, 0)
    y = cw[0:1, :] * r2 + cw[1:2, :] * r1 + cw[2:3, :] * a_ext
    return y[CONV_HALO:, :] + cb


def _ffn_up_kernel(u_ref, wg_ref, wu_ref, cwg_ref, cwu_ref, cbg_ref, cbu_ref, *rest, tm, seq, n_side):
    side_in, rest = rest[:n_side], rest[n_side:]
    o_ref, side_out = rest[0], rest[1:1 + n_side]
    wgb_ref, wub_ref, carry_ref = rest[1 + n_side:]
    i = pl.program_id(1)

    for src_ref, dst_ref in zip(side_in, side_out):
        dst_ref[...] = src_ref[...].astype(dst_ref.dtype)

    @pl.when(i == 0)
    def _():
        wgb_ref[...] = wg_ref[...].astype(BF16)
        wub_ref[...] = wu_ref[...].astype(BF16)

    @pl.when((i * tm) % seq == 0)
    def _():
        carry_ref[...] = jnp.zeros(carry_ref.shape, F32)

    u = u_ref[...]
    act = []
    for idx, (wb_ref, cw_ref, cb_ref) in enumerate(((wgb_ref, cwg_ref, cbg_ref), (wub_ref, cwu_ref, cbu_ref))):
        a = jnp.dot(u, wb_ref[...], preferred_element_type=F32)
        act.append(_causal_conv3(a, carry_ref[idx], cw_ref[...], cb_ref[...]))
        carry_ref[idx] = a[tm - CONV_HALO:, :]
    o_ref[...] = (_gelu_tanh(act[0]) * act[1]).astype(o_ref.dtype)


def ffn_up(u, w_up, layer, conv_w, conv_b, seq, side_casts=(), *, tm=1024, tn=512):
    m, d = u.shape
    f = w_up.shape[2] // 2
    nn = f // tn
    nm = m // tm
    assert seq % tm == 0
    cb = conv_b.reshape(1, 2 * f)
    side_specs, side_shapes = [], []
    for w, l in side_casts:
        rows = w.shape[1] // (nn * nm)
        assert rows * nn * nm == w.shape[1] and rows % 16 == 0
        side_specs.append(pl.BlockSpec((None, rows, w.shape[2]), lambda j, i, l=l: (l, j * nm + i, 0)))
        side_shapes.append(jax.ShapeDtypeStruct((1,) + w.shape[1:], BF16))
    side_out_specs = [pl.BlockSpec(s.block_shape, lambda j, i: (0, j * nm + i, 0)) for s in side_specs]
    outs = pl.pallas_call(
        functools.partial(_ffn_up_kernel, tm=tm, seq=seq, n_side=len(side_casts)),
        grid=(nn, nm),
        in_specs=[
            pl.BlockSpec((tm, d), lambda j, i: (i, 0)),
            pl.BlockSpec((None, d, tn), lambda j, i: (layer, 0, j)),
            pl.BlockSpec((None, d, tn), lambda j, i: (layer, 0, nn + j)),
            pl.BlockSpec((CONV_WIDTH, tn), lambda j, i: (0, j)),
            pl.BlockSpec((CONV_WIDTH, tn), lambda j, i: (0, nn + j)),
            pl.BlockSpec((1, tn), lambda j, i: (0, j)),
            pl.BlockSpec((1, tn), lambda j, i: (0, nn + j)),
        ] + side_specs,
        out_specs=[pl.BlockSpec((tm, tn), lambda j, i: (i, j))] + side_out_specs,
        out_shape=[jax.ShapeDtypeStruct((m, f), BF16)] + side_shapes,
        scratch_shapes=[pltpu.VMEM((d, tn), BF16), pltpu.VMEM((d, tn), BF16),
                        pltpu.VMEM((2, CONV_HALO, tn), F32)],
        compiler_params=_params(("arbitrary", "arbitrary")),
        name="ffn_up_conv_glu",
    )(u, w_up, w_up, conv_w, conv_w, cb, cb, *[w for w, _ in side_casts])
    return outs


def _rel_bucket_np(dist):
    dist = np.maximum(dist, 0)
    max_exact = N_BUCKETS // 2
    d = np.maximum(dist, 1).astype(np.float32)
    large = max_exact + (np.log(d / max_exact) / math.log(MAX_DISTANCE / max_exact)
                         * (N_BUCKETS - max_exact)).astype(np.int32)
    large = np.minimum(large, N_BUCKETS - 1)
    return np.where(dist < max_exact, dist, large)


def _strip_distances(n_delta, t):
    return np.arange(n_delta)[:, None] * t + np.arange(2 * t)[None, :] - t


def _causal_log_mult(dist):
    return np.where(dist >= 0, 0.0, -np.inf)


def _dilated_log_mult(dist):
    mult = np.zeros(dist.shape, np.float64)
    for window, dil in DILATED_GROUPS:
        mult += (dist >= 0) & (dist <= window) & (dist % dil == 0)
    with np.errstate(divide="ignore"):
        return np.log(mult)


def _bias_strips(rel_bias, dist, log_mult):
    shown = np.isfinite(log_mult)
    bucket = _rel_bucket_np(dist)
    last_uniform = bool(shown[-1].all() and (log_mult[-1] == log_mult[-1, 0]).all()
                        and (bucket[-1] == bucket[-1, 0]).all())
    offset = jnp.asarray(np.where(shown, log_mult * LOG2E, NEG), F32)
    if rel_bias is None:
        return offset[None], last_uniform
    bias = jnp.take(rel_bias.astype(F32), jnp.asarray(bucket), axis=0)
    strips = jnp.where(jnp.asarray(shown)[..., None], bias * LOG2E + offset[..., None], NEG)
    return jnp.moveaxis(strips, -1, 0), last_uniform


def _attn_kernel(*refs, kind, g, t, seq, n_delta, last_uniform, lambda_init):
    n_in = {"dilated": 4, "moba": 4, "diff": 6, "fox": 6}[kind]
    q_ref, k_ref, v_ref, strip_ref = refs[:4]
    o_ref = refs[n_in]
    tab_ref, vt_ref, s_ref = refs[n_in + 1:n_in + 4]
    hg = pl.program_id(0)
    nk = seq // t
    n_tab = tab_ref.shape[0]
    n_maps = s_ref.shape[0] // (2 * g)
    dh = HEAD_DIM

    @pl.when(pl.program_id(1) == 0)
    def _():
        for hh in range(n_tab):
            for dl in range(n_delta):
                strip = jnp.broadcast_to(strip_ref[hh, dl:dl + 1, :], (t, 2 * t))
                tab_ref[hh, dl] = pltpu.roll(strip, 0, 1, stride=1, stride_axis=0)[:, t:]

    def tile(j):
        return slice(j * t, (j + 1) * t)

    def head_passes(hh):
        head = slice(hh * dh, (hh + 1) * dh)
        ones_row = lax.broadcasted_iota(jnp.int32, (DEN_ROWS, seq), 0) == 0
        vt_ref[hh, dh:, :] = jnp.where(ones_row, 1.0, 0.0).astype(BF16)
        for jb in range(nk):
            vt_ref[hh, :dh, tile(jb)] = v_ref[tile(jb), head].T
        if kind == "moba":
            kmean_ref = refs[n_in + 4]
            for jb in range(nk):
                kmean_ref[hh, jb:jb + 1, :] = jnp.mean(k_ref[tile(jb), head].astype(F32), axis=0, keepdims=True)
            km = kmean_ref[hh]
            km_hi = km.astype(BF16)
            km_lo = (km - km_hi.astype(F32)).astype(BF16)
        if kind == "fox":
            cs_ref, cst_ref, ckb_ref = refs[4], refs[5], refs[n_in + 4]
            cs = cs_ref[...]
            lane = lax.broadcasted_iota(jnp.int32, cs.shape, 1)
            ck = jnp.sum(jnp.where(lane == hg * g + hh, cs, 0.0), axis=-1, keepdims=True)
            ckb_ref[hh] = jnp.broadcast_to(ck, cs.shape)
        if kind == "diff":
            lam_ref, sub_ref = refs[4], refs[5]
            lam = lam_ref[...]
            lmbda = (jnp.exp(jnp.sum(lam[0:1, :] * lam[1:2, :], axis=-1, keepdims=True))
                     - jnp.exp(jnp.sum(lam[2:3, :] * lam[3:4, :], axis=-1, keepdims=True)) + lambda_init)
        far_term = strip_ref[hh % n_tab, n_delta - 1:n_delta, 0:1] if last_uniform else None

        def is_far(i, j):
            return last_uniform and i - j >= n_delta - 1

        def score_pass(i):
            qt = q_ref[tile(i), head].T
            if kind == "diff":
                row = lax.broadcasted_iota(jnp.int32, qt.shape, 0)
                qts = [jnp.where(row < DIFF_HEAD_DIM, qt, jnp.zeros_like(qt)),
                       jnp.where(row >= DIFF_HEAD_DIM, qt, jnp.zeros_like(qt))]
            else:
                qts = [qt]
            sel = None
            if kind == "moba" and i > 0:
                gate = (jnp.dot(km_hi, qt, preferred_element_type=F32)
                        + jnp.dot(km_lo, qt, preferred_element_type=F32))
                blk = lax.broadcasted_iota(jnp.int32, gate.shape, 0)
                sel = []
                for jb in range(i):
                    gj = gate[jb:jb + 1, :]
                    beats = (blk < i) & ((gate > gj) | ((gate == gj) & (blk < jb)))
                    rank = jnp.sum(jnp.where(beats, 1.0, 0.0), axis=0, keepdims=True)
                    sel.append(jnp.where(rank < MOBA_TOPK, 0.0, NEG))
            maxima = []
            for mp in range(n_maps):
                buf = (hh * 2 + i % 2) * n_maps + mp
                m = None
                for j in range(i + 1):
                    s = jnp.dot(k_ref[tile(j), head], qts[mp], preferred_element_type=F32)
                    if kind == "fox":
                        s = s + cst_ref[hh, i:i + 1, :] - jnp.concatenate([ckb_ref[hh, tile(j), :]] * (t // dh), axis=1)
                    if not is_far(i, j):
                        s = s + tab_ref[hh % n_tab, min(i - j, n_delta - 1)]
                    if kind == "moba" and j < i:
                        s = s + sel[j]
                    s_ref[buf, tile(j), :] = s
                    tile_max = jnp.max(s, axis=0, keepdims=True)
                    if is_far(i, j):
                        tile_max = tile_max + far_term
                    m = tile_max if m is None else jnp.maximum(m, tile_max)
                maxima.append(m)
            return maxima

        def value_pass(i, maxima):
            outs = []
            for mp in range(n_maps):
                buf = (hh * 2 + i % 2) * n_maps + mp
                acc = None
                for j in range(i + 1):
                    shift = maxima[mp] - far_term if is_far(i, j) else maxima[mp]
                    p = jnp.exp2(s_ref[buf, tile(j), :] - shift).astype(BF16)
                    part = jnp.dot(vt_ref[hh, :, tile(j)], p, preferred_element_type=F32)
                    acc = part if acc is None else acc + part
                outs.append(acc[:dh, :] / acc[dh:dh + 1, :])
            if kind == "diff":
                out = _rms_rows((outs[0] - lmbda * outs[1]).T, sub_ref[...]) * (1.0 - lambda_init)
            else:
                out = outs[0].T
            o_ref[tile(i), head] = out.astype(o_ref.dtype)

        return score_pass, value_pass

    passes = [head_passes(hh) for hh in range(g)]
    maxima = [score_pass(0) for score_pass, _ in passes]
    for i in range(nk):
        next_maxima = [score_pass(i + 1) for score_pass, _ in passes] if i + 1 < nk else None
        for (_, value_pass), head_maxima in zip(passes, maxima):
            value_pass(i, head_maxima)
        maxima = next_maxima


def attention(kind, qkv, strips, last_uniform, batch, seq, extra=(), lambda_init=0.0):
    t, dh = ATTN_TILE, HEAD_DIM
    g = ATTN_STREAMS_PER_STEP // (2 if kind == "diff" else 1)
    ng = N_HEADS // g
    m = qkv.shape[0]
    per_head = strips.shape[0] == N_HEADS
    n_tab = g if per_head else 1
    n_delta = strips.shape[1]
    n_maps = 2 if kind == "diff" else 1
    in_specs = [
        pl.BlockSpec((seq, g * dh), lambda h, b: (b, h)),
        pl.BlockSpec((seq, g * dh), lambda h, b: (b, ng + h)),
        pl.BlockSpec((seq, g * dh), lambda h, b: (b, 2 * ng + h)),
        pl.BlockSpec((n_tab, n_delta, 2 * t), (lambda h, b: (h, 0, 0)) if per_head else (lambda h, b: (0, 0, 0))),
    ]
    args = [qkv, qkv, qkv, strips]
    scratch = [
        pltpu.VMEM((n_tab, n_delta, t, t), F32),
        pltpu.VMEM((g, dh + DEN_ROWS, seq), BF16),
        pltpu.VMEM((g * 2 * n_maps, seq, t), F32),
    ]
    if kind == "moba":
        scratch.append(pltpu.VMEM((g, seq // t, dh), F32))
    elif kind == "diff":
        lam, subln = extra
        in_specs += [pl.BlockSpec(lam.shape, lambda h, b: (0, 0)),
                     pl.BlockSpec((1, dh), lambda h, b: (0, 0))]
        args += [lam, subln.reshape(1, dh)]
    elif kind == "fox":
        csum, csum_t = extra
        in_specs += [pl.BlockSpec((None, seq, dh), lambda h, b: (b, 0, 0)),
                     pl.BlockSpec((None, g, seq // t, t), lambda h, b: (b, h, 0, 0))]
        args += [csum, csum_t]
        scratch.append(pltpu.VMEM((g, seq, dh), F32))
    return pl.pallas_call(
        functools.partial(_attn_kernel, kind=kind, g=g, t=t, seq=seq, n_delta=n_delta,
                          last_uniform=last_uniform, lambda_init=lambda_init),
        grid=(ng, batch),
        in_specs=in_specs,
        out_specs=pl.BlockSpec((seq, g * dh), lambda h, b: (b, h)),
        out_shape=jax.ShapeDtypeStruct((m, N_HEADS * dh), BF16),
        scratch_shapes=scratch,
        compiler_params=_params(("parallel", "arbitrary")),
        name=kind + "_attention",
    )(*args)


def _fox_gate_kernel(f_ref, b_ref, cs_ref, cst_ref):
    z = f_ref[...] + b_ref[...]
    x = jnp.minimum(z, 0.0) - jnp.log1p(jnp.exp(-jnp.abs(z)))
    row = lax.broadcasted_iota(jnp.int32, x.shape, 0)
    shift = 1
    while shift < x.shape[0]:
        x = x + jnp.where(row >= shift, pltpu.roll(x, shift, 0), 0.0)
        shift *= 2
    x = x * LOG2E
    cs_ref[...] = x
    cst_ref[...] = x.T


def fox_gate(f, b_pad):
    batch, seq, lanes = f.shape
    return pl.pallas_call(
        _fox_gate_kernel,
        grid=(batch,),
        in_specs=[pl.BlockSpec((None, seq, lanes), lambda b: (b, 0, 0)),
                  pl.BlockSpec((1, lanes), lambda b: (0, 0))],
        out_specs=[pl.BlockSpec((None, seq, lanes), lambda b: (b, 0, 0)),
                   pl.BlockSpec((None, lanes, seq), lambda b: (b, 0, 0))],
        out_shape=[jax.ShapeDtypeStruct((batch, seq, lanes), F32),
                   jax.ShapeDtypeStruct((batch, lanes, seq), F32)],
        compiler_params=_params(("parallel",)),
        name="fox_gate_cumsum",
    )(f, b_pad)


def kernel(x, rel_bias, norm_gains, w_out, w_up, conv_w, conv_b, w_down, w_in_dil, w_in_moba,
           w_in_diff, lambda_diff, subln_diff, w_in_fox, b_fox):
    batch, seq, d = x.shape
    depth = norm_gains.shape[0]
    t = ATTN_TILE
    assert seq % t == 0 and t == MOBA_BLOCK and d == N_HEADS * HEAD_DIM
    causal3 = _strip_distances(3, t)
    causal_strips, causal_uniform = _bias_strips(rel_bias, causal3, _causal_log_mult(causal3))

    def qkv_scale(head_scale):
        return jnp.concatenate([jnp.full((d,), head_scale * LOG2E, F32), jnp.ones((2 * d,), F32)])

    w_out_b = cast_bf16(w_out, 1)

    h = x.reshape(batch * seq, d)
    u = rms_norm_bf16(h, norm_gains[0, 0])
    for i in range(depth):
        mixer, j = i % N_MIXERS, i // N_MIXERS
        g = norm_gains[i]
        if mixer == 0:
            qkv = proj_matmul(u, w_in_dil, j, qkv_scale(HEAD_DIM ** -0.5))
            dist = _strip_distances(4, t)
            strips, uniform = _bias_strips(rel_bias, dist, _dilated_log_mult(dist))
            y = attention("dilated", qkv, strips, uniform, batch, seq)
        elif mixer == 1:
            qkv = proj_matmul(u, w_in_moba, j, qkv_scale(HEAD_DIM ** -0.5))
            y = attention("moba", qkv, causal_strips, causal_uniform, batch, seq)
        elif mixer == 2:
            qkv = proj_matmul(u, w_in_diff, j, qkv_scale(DIFF_HEAD_DIM ** -0.5))
            y = attention("diff", qkv, causal_strips, causal_uniform, batch, seq,
                          extra=(lambda_diff[j], subln_diff[j]), lambda_init=0.8 - 0.6 * math.exp(-0.3 * i))
        else:
            w_fox_t = jnp.swapaxes(w_in_fox, 1, 2)
            qkv = proj_matmul(u, w_fox_t, j, qkv_scale(HEAD_DIM ** -0.5), w_transposed=True)
            w_f = jnp.pad(w_fox_t[j, 3 * d:, :], ((0, HEAD_DIM - N_HEADS), (0, 0)))[None]
            f = proj_matmul(u, w_f, 0, jnp.ones((HEAD_DIM,), F32), w_transposed=True, out_dtype=F32)
            b_pad = jnp.pad(b_fox[j], (0, HEAD_DIM - N_HEADS)).reshape(1, HEAD_DIM)
            csum, csum_t = fox_gate(f.reshape(batch, seq, HEAD_DIM), b_pad)
            dist = _strip_distances(2, t)
            strips, uniform = _bias_strips(None, dist, _causal_log_mult(dist))
            y = attention("fox", qkv, strips, uniform, batch, seq,
                          extra=(csum, csum_t.reshape(batch, HEAD_DIM, seq // t, t)))
        h, u = matmul_norm_res(y, w_out_b, 0, g[1], h, g[2], tk=d)
        side = [(w_down, i)] + ([(w_out, i + 1)] if i + 1 < depth else [])
        hidden, w_down_b, *w_out_next = ffn_up(u, w_up, i, conv_w[i], conv_b[i], seq, side)
        if w_out_next:
            w_out_b = w_out_next[0]
        h, u = matmul_norm_res(hidden, w_down_b, 0, g[3], h, norm_gains[i + 1, 0] if i + 1 < depth else None)
    return h.reshape(batch, seq, d)
```

```python
import functools
import math

import jax
import jax.numpy as jnp
import numpy as np
from jax import lax
from jax.experimental import pallas as pl
from jax.experimental.pallas import tpu as pltpu

N_HEADS = 16
HEAD_DIM = 128
N_MIXERS = 4
CONV_WIDTH = 3
RMS_EPS = 1e-6
N_BUCKETS = 32
MAX_DISTANCE = 128
DILATED_GROUPS = ((128, 1), (512, 4), (2048, 16))
MOBA_BLOCK = 256
MOBA_TOPK = 3
DIFF_HEAD_DIM = HEAD_DIM // 2
NEG = -1e30
LOG2E = math.log2(math.e)

ATTN_TILE = 256
ATTN_STREAMS_PER_STEP = 4
SCORE_LOOKAHEAD = 1
SCORE_BUFFERS = SCORE_LOOKAHEAD + 1
DEN_ROWS = 16
CONV_HALO = 16
FINISH_ROW_CHUNK = 128
CAST_BLOCK_ELEMS = 2 * 1024 * 1024
VMEM_LIMIT = 52 * 1024 * 1024

BF16 = jnp.bfloat16
F32 = jnp.float32


def _rms_rows(x, g):
    return x * lax.rsqrt(jnp.mean(x * x, axis=-1, keepdims=True) + RMS_EPS) * g


def _params(sem, vmem=VMEM_LIMIT):
    return pltpu.CompilerParams(dimension_semantics=sem, vmem_limit_bytes=vmem)


def _cast_kernel(x_ref, o_ref):
    o_ref[...] = x_ref[...].astype(o_ref.dtype)


def cast_bf16(w, layers):
    _, k, n = w.shape
    rows = 1 << (min(k, CAST_BLOCK_ELEMS // n).bit_length() - 1)
    assert k % rows == 0
    spec = pl.BlockSpec((None, rows, n), lambda l, r: (l, r, 0))
    return pl.pallas_call(
        _cast_kernel,
        grid=(layers, k // rows),
        in_specs=[spec],
        out_specs=spec,
        out_shape=jax.ShapeDtypeStruct((layers, k, n), BF16),
        compiler_params=_params(("parallel", "parallel")),
        name="cast_bf16",
    )(w)


def _rms_norm_kernel(x_ref, g_ref, o_ref):
    o_ref[...] = _rms_rows(x_ref[...], g_ref[...]).astype(o_ref.dtype)


def rms_norm_bf16(x, g, *, tm=512):
    m, d = x.shape
    return pl.pallas_call(
        _rms_norm_kernel,
        grid=(m // tm,),
        in_specs=[pl.BlockSpec((tm, d), lambda i: (i, 0)), pl.BlockSpec((1, d), lambda i: (0, 0))],
        out_specs=pl.BlockSpec((tm, d), lambda i: (i, 0)),
        out_shape=jax.ShapeDtypeStruct((m, d), BF16),
        compiler_params=_params(("parallel",)),
        name="rms_norm_bf16",
    )(x, g.reshape(1, d))


def _proj_kernel(u_ref, w_ref, cs_ref, o_ref, wb_ref, *, w_transposed):
    @pl.when(pl.program_id(1) == 0)
    def _():
        wb = w_ref[...].astype(BF16)
        wb_ref[...] = wb.T if w_transposed else wb

    acc = jnp.dot(u_ref[...], wb_ref[...], preferred_element_type=F32)
    o_ref[...] = (acc * cs_ref[...]).astype(o_ref.dtype)


def proj_matmul(u, w, layer, col_scale, *, w_transposed=False, out_dtype=BF16, tm=1024, tn=1024):
    m, d = u.shape
    n = col_scale.shape[0]
    tn = min(tn, n)
    w_spec = (pl.BlockSpec((None, tn, d), lambda j, i: (layer, j, 0)) if w_transposed
              else pl.BlockSpec((None, d, tn), lambda j, i: (layer, 0, j)))
    return pl.pallas_call(
        functools.partial(_proj_kernel, w_transposed=w_transposed),
        grid=(n // tn, m // tm),
        in_specs=[
            pl.BlockSpec((tm, d), lambda j, i: (i, 0)),
            w_spec,
            pl.BlockSpec((1, tn), lambda j, i: (0, j)),
        ],
        out_specs=pl.BlockSpec((tm, tn), lambda j, i: (i, j)),
        out_shape=jax.ShapeDtypeStruct((m, n), out_dtype),
        scratch_shapes=[pltpu.VMEM((d, tn), BF16)],
        compiler_params=_params(("parallel", "arbitrary")),
        name="proj_matmul",
    )(u, w, col_scale.reshape(1, n))


def _matmul_norm_res_kernel(y_ref, w_ref, g_ref, res_ref, *rest, has_partial, has_next):
    rest = list(rest)
    part_ref = rest.pop(0) if has_partial else None
    gn_ref = rest.pop(0) if has_next else None
    o_ref = rest.pop(0)
    u_ref = rest.pop(0) if has_next else None
    for r in range(0, o_ref.shape[0], FINISH_ROW_CHUNK):
        rows = slice(r, r + FINISH_ROW_CHUNK)
        acc = jnp.dot(y_ref[rows, :], w_ref[...], preferred_element_type=F32)
        if has_partial:
            acc = acc + part_ref[rows, :]
        h = res_ref[rows, :] + _rms_rows(acc, g_ref[...])
        o_ref[rows, :] = h
        if has_next:
            u_ref[rows, :] = _rms_rows(h, gn_ref[...]).astype(u_ref.dtype)


def _resident_weight_specs(w, layer, k_block, tm, tk):
    return [pl.BlockSpec((tm, tk), lambda i: (i, k_block)),
            pl.BlockSpec((None, tk, w.shape[2]), lambda i: (layer, k_block, 0), pipeline_mode=pl.Buffered(1))]


def matmul_norm_res(y, w, layer, g, res, g_next=None, *, partial=None, k_block=0, tk=None, tm=512):
    m = y.shape[0]
    d = w.shape[2]
    tk = y.shape[1] if tk is None else tk
    has_partial, has_next = partial is not None, g_next is not None
    row_spec = pl.BlockSpec((tm, d), lambda i: (i, 0))
    gain_spec = pl.BlockSpec((1, d), lambda i: (0, 0))
    in_specs = _resident_weight_specs(w, layer, k_block, tm, tk) + [gain_spec, row_spec]
    args = [y, w, g.reshape(1, d), res]
    out_shape = [jax.ShapeDtypeStruct((m, d), F32)]
    out_specs = [row_spec]
    if has_partial:
        in_specs.append(row_spec)
        args.append(partial)
    if has_next:
        in_specs.append(gain_spec)
        args.append(g_next.reshape(1, d))
        out_shape.append(jax.ShapeDtypeStruct((m, d), BF16))
        out_specs.append(row_spec)
    outs = pl.pallas_call(
        functools.partial(_matmul_norm_res_kernel, has_partial=has_partial, has_next=has_next),
        grid=(m // tm,),
        in_specs=in_specs,
        out_specs=out_specs,
        out_shape=out_shape,
        compiler_params=_params(("parallel",)),
        name="matmul_norm_res",
    )(*args)
    return outs if has_next else (outs[0], None)


def _partial_matmul_kernel(y_ref, w_ref, o_ref):
    o_ref[...] = jnp.dot(y_ref[...], w_ref[...], preferred_element_type=F32)


def partial_matmul(y, w, layer, k_block, tk, *, tm=512):
    m = y.shape[0]
    d = w.shape[2]
    return pl.pallas_call(
        _partial_matmul_kernel,
        grid=(m // tm,),
        in_specs=_resident_weight_specs(w, layer, k_block, tm, tk),
        out_specs=pl.BlockSpec((tm, d), lambda i: (i, 0)),
        out_shape=jax.ShapeDtypeStruct((m, d), F32),
        compiler_params=_params(("parallel",)),
        name="partial_matmul",
    )(y, w)


def _gelu_tanh(x):
    return x * (0.5 * (1.0 + jnp.tanh(math.sqrt(2.0 / math.pi) * (x + 0.044715 * (x * x * x)))))


def _causal_conv3(a, tail, cw, cb):
    a_ext = jnp.concatenate([tail, a], axis=0)
    r1 = pltpu.roll(a_ext, 1, 0)
    r2 = pltpu.roll(a_ext, 2, 0)
    y = cw[0:1, :] * r2 + cw[1:2, :] * r1 + cw[2:3, :] * a_ext
    return y[CONV_HALO:, :] + cb


def _ffn_up_kernel(u_ref, wg_ref, wu_ref, cwg_ref, cwu_ref, cbg_ref, cbu_ref, *rest, tm, seq, n_side):
    side_in, rest = rest[:n_side], rest[n_side:]
    o_ref, side_out = rest[0], rest[1:1 + n_side]
    wgb_ref, wub_ref, carry_ref = rest[1 + n_side:]
    i = pl.program_id(1)

    for src_ref, dst_ref in zip(side_in, side_out):
        dst_ref[...] = src_ref[...].astype(dst_ref.dtype)

    @pl.when(i == 0)
    def _():
        wgb_ref[...] = wg_ref[...].astype(BF16)
        wub_ref[...] = wu_ref[...].astype(BF16)

    @pl.when((i * tm) % seq == 0)
    def _():
        carry_ref[...] = jnp.zeros(carry_ref.shape, F32)

    u = u_ref[...]
    act = []
    for idx, (wb_ref, cw_ref, cb_ref) in enumerate(((wgb_ref, cwg_ref, cbg_ref), (wub_ref, cwu_ref, cbu_ref))):
        a = jnp.dot(u, wb_ref[...], preferred_element_type=F32)
        act.append(_causal_conv3(a, carry_ref[idx], cw_ref[...], cb_ref[...]))
        carry_ref[idx] = a[tm - CONV_HALO:, :]
    o_ref[...] = (_gelu_tanh(act[0]) * act[1]).astype(o_ref.dtype)


def ffn_up(u, w_up, layer, conv_w, conv_b, seq, side_casts=(), *, tm=1024, tn=512):
    m, d = u.shape
    f = w_up.shape[2] // 2
    nn = f // tn
    nm = m // tm
    assert seq % tm == 0
    cb = conv_b.reshape(1, 2 * f)
    side_specs, side_shapes = [], []
    for w, l in side_casts:
        rows = w.shape[1] // (nn * nm)
        assert rows * nn * nm == w.shape[1] and rows % 16 == 0
        side_specs.append(pl.BlockSpec((None, rows, w.shape[2]), lambda j, i, l=l: (l, j * nm + i, 0)))
        side_shapes.append(jax.ShapeDtypeStruct((1,) + w.shape[1:], BF16))
    side_out_specs = [pl.BlockSpec(s.block_shape, lambda j, i: (0, j * nm + i, 0)) for s in side_specs]
    outs = pl.pallas_call(
        functools.partial(_ffn_up_kernel, tm=tm, seq=seq, n_side=len(side_casts)),
        grid=(nn, nm),
        in_specs=[
            pl.BlockSpec((tm, d), lambda j, i: (i, 0)),
            pl.BlockSpec((None, d, tn), lambda j, i: (layer, 0, j)),
            pl.BlockSpec((None, d, tn), lambda j, i: (layer, 0, nn + j)),
            pl.BlockSpec((CONV_WIDTH, tn), lambda j, i: (0, j)),
            pl.BlockSpec((CONV_WIDTH, tn), lambda j, i: (0, nn + j)),
            pl.BlockSpec((1, tn), lambda j, i: (0, j)),
            pl.BlockSpec((1, tn), lambda j, i: (0, nn + j)),
        ] + side_specs,
        out_specs=[pl.BlockSpec((tm, tn), lambda j, i: (i, j))] + side_out_specs,
        out_shape=[jax.ShapeDtypeStruct((m, f), BF16)] + side_shapes,
        scratch_shapes=[pltpu.VMEM((d, tn), BF16), pltpu.VMEM((d, tn), BF16),
                        pltpu.VMEM((2, CONV_HALO, tn), F32)],
        compiler_params=_params(("arbitrary", "arbitrary")),
        name="ffn_up_conv_glu",
    )(u, w_up, w_up, conv_w, conv_w, cb, cb, *[w for w, _ in side_casts])
    return outs


def _rel_bucket_np(dist):
    dist = np.maximum(dist, 0)
    max_exact = N_BUCKETS // 2
    d = np.maximum(dist, 1).astype(np.float32)
    large = max_exact + (np.log(d / max_exact) / math.log(MAX_DISTANCE / max_exact)
                         * (N_BUCKETS - max_exact)).astype(np.int32)
    large = np.minimum(large, N_BUCKETS - 1)
    return np.where(dist < max_exact, dist, large)


def _strip_distances(n_delta, t):
    return np.arange(n_delta)[:, None] * t + np.arange(2 * t)[None, :] - t


def _causal_log_mult(dist):
    return np.where(dist >= 0, 0.0, -np.inf)


def _dilated_log_mult(dist):
    mult = np.zeros(dist.shape, np.float64)
    for window, dil in DILATED_GROUPS:
        mult += (dist >= 0) & (dist <= window) & (dist % dil == 0)
    with np.errstate(divide="ignore"):
        return np.log(mult)


def _bias_strips(rel_bias, dist, log_mult):
    shown = np.isfinite(log_mult)
    bucket = _rel_bucket_np(dist)
    last_uniform = bool(shown[-1].all() and (log_mult[-1] == log_mult[-1, 0]).all()
                        and (bucket[-1] == bucket[-1, 0]).all())
    offset = jnp.asarray(np.where(shown, log_mult * LOG2E, NEG), F32)
    if rel_bias is None:
        return offset[None], last_uniform
    bias = jnp.take(rel_bias.astype(F32), jnp.asarray(bucket), axis=0)
    strips = jnp.where(jnp.asarray(shown)[..., None], bias * LOG2E + offset[..., None], NEG)
    return jnp.moveaxis(strips, -1, 0), last_uniform


def _attn_kernel(*refs, kind, g, t, seq, n_delta, last_uniform, lambda_init):
    n_in = {"dilated": 4, "moba": 4, "diff": 6, "fox": 6}[kind]
    q_ref, k_ref, v_ref, strip_ref = refs[:4]
    o_ref = refs[n_in]
    tab_ref, vt_ref, s_ref = refs[n_in + 1:n_in + 4]
    hg = pl.program_id(0)
    nk = seq // t
    n_tab = tab_ref.shape[0]
    n_maps = s_ref.shape[0] // (SCORE_BUFFERS * g)
    dh = HEAD_DIM

    @pl.when(pl.program_id(1) == 0)
    def _():
        for hh in range(n_tab):
            for dl in range(n_delta):
                strip = jnp.broadcast_to(strip_ref[hh, dl:dl + 1, :], (t, 2 * t))
                tab_ref[hh, dl] = pltpu.roll(strip, 0, 1, stride=1, stride_axis=0)[:, t:]

    def tile(j):
        return slice(j * t, (j + 1) * t)

    def head_passes(hh):
        head = slice(hh * dh, (hh + 1) * dh)
        ones_row = lax.broadcasted_iota(jnp.int32, (DEN_ROWS, seq), 0) == 0
        vt_ref[hh, dh:, :] = jnp.where(ones_row, 1.0, 0.0).astype(BF16)
        for jb in range(nk):
            vt_ref[hh, :dh, tile(jb)] = v_ref[tile(jb), head].T
        if kind == "moba":
            kmean_ref = refs[n_in + 4]
            for jb in range(nk):
                kmean_ref[hh, jb:jb + 1, :] = jnp.mean(k_ref[tile(jb), head].astype(F32), axis=0, keepdims=True)
            km = kmean_ref[hh]
            km_hi = km.astype(BF16)
            km_lo = (km - km_hi.astype(F32)).astype(BF16)
        if kind == "fox":
            cs_ref, cst_ref, ckb_ref = refs[4], refs[5], refs[n_in + 4]
            cs = cs_ref[...]
            lane = lax.broadcasted_iota(jnp.int32, cs.shape, 1)
            ck = jnp.sum(jnp.where(lane == hg * g + hh, cs, 0.0), axis=-1, keepdims=True)
            ckb_ref[hh] = jnp.broadcast_to(ck, cs.shape)
        if kind == "diff":
            lam_ref, sub_ref = refs[4], refs[5]
            lam = lam_ref[...]
            lmbda = (jnp.exp(jnp.sum(lam[0:1, :] * lam[1:2, :], axis=-1, keepdims=True))
                     - jnp.exp(jnp.sum(lam[2:3, :] * lam[3:4, :], axis=-1, keepdims=True)) + lambda_init)
        far_term = strip_ref[hh % n_tab, n_delta - 1:n_delta, 0:1] if last_uniform else None

        def is_far(i, j):
            return last_uniform and i - j >= n_delta - 1

        def score_pass(i):
            qt = q_ref[tile(i), head].T
            if kind == "diff":
                row = lax.broadcasted_iota(jnp.int32, qt.shape, 0)
                qts = [jnp.where(row < DIFF_HEAD_DIM, qt, jnp.zeros_like(qt)),
                       jnp.where(row >= DIFF_HEAD_DIM, qt, jnp.zeros_like(qt))]
            else:
                qts = [qt]
            sel = None
            if kind == "moba" and i > 0:
                gate = (jnp.dot(km_hi, qt, preferred_element_type=F32)
                        + jnp.dot(km_lo, qt, preferred_element_type=F32))
                blk = lax.broadcasted_iota(jnp.int32, gate.shape, 0)
                sel = []
                for jb in range(i):
                    gj = gate[jb:jb + 1, :]
                    beats = (blk < i) & ((gate > gj) | ((gate == gj) & (blk < jb)))
                    rank = jnp.sum(jnp.where(beats, 1.0, 0.0), axis=0, keepdims=True)
                    sel.append(jnp.where(rank < MOBA_TOPK, 0.0, NEG))
            maxima = []
            for mp in range(n_maps):
                buf = (hh * SCORE_BUFFERS + i % SCORE_BUFFERS) * n_maps + mp
                m = None
                for j in range(i + 1):
                    s = jnp.dot(k_ref[tile(j), head], qts[mp], preferred_element_type=F32)
                    if kind == "fox":
                        s = s + cst_ref[hh, i:i + 1, :] - jnp.concatenate([ckb_ref[hh, tile(j), :]] * (t // dh), axis=1)
                    if not is_far(i, j):
                        s = s + tab_ref[hh % n_tab, min(i - j, n_delta - 1)]
                    if kind == "moba" and j < i:
                        s = s + sel[j]
                    s_ref[buf, tile(j), :] = s
                    tile_max = jnp.max(s, axis=0, keepdims=True)
                    if is_far(i, j):
                        tile_max = tile_max + far_term
                    m = tile_max if m is None else jnp.maximum(m, tile_max)
                maxima.append(m)
            return maxima

        def value_pass(i, maxima):
            outs = []
            for mp in range(n_maps):
                buf = (hh * SCORE_BUFFERS + i % SCORE_BUFFERS) * n_maps + mp
                acc = None
                for j in range(i + 1):
                    shift = maxima[mp] - far_term if is_far(i, j) else maxima[mp]
                    p = jnp.exp2(s_ref[buf, tile(j), :] - shift).astype(BF16)
                    part = jnp.dot(vt_ref[hh, :, tile(j)], p, preferred_element_type=F32)
                    acc = part if acc is None else acc + part
                outs.append(acc[:dh, :] / acc[dh:dh + 1, :])
            if kind == "diff":
                out = _rms_rows((outs[0] - lmbda * outs[1]).T, sub_ref[...]) * (1.0 - lambda_init)
            else:
                out = outs[0].T
            o_ref[tile(i), head] = out.astype(o_ref.dtype)

        return score_pass, value_pass

    passes = [head_passes(hh) for hh in range(g)]
    maxima = {}
    for i in range(-SCORE_LOOKAHEAD, nk):
        ahead = i + SCORE_LOOKAHEAD
        if ahead < nk:
            for hh, (score_pass, _) in enumerate(passes):
                maxima[hh, ahead] = score_pass(ahead)
        if i >= 0:
            for hh, (_, value_pass) in enumerate(passes):
                value_pass(i, maxima.pop((hh, i)))


def attention(kind, qkv, strips, last_uniform, batch, seq, extra=(), lambda_init=0.0):
    t, dh = ATTN_TILE, HEAD_DIM
    g = ATTN_STREAMS_PER_STEP // (2 if kind == "diff" else 1)
    ng = N_HEADS // g
    m = qkv.shape[0]
    per_head = strips.shape[0] == N_HEADS
    n_tab = g if per_head else 1
    n_delta = strips.shape[1]
    n_maps = 2 if kind == "diff" else 1
    in_specs = [
        pl.BlockSpec((seq, g * dh), lambda h, b: (b, h)),
        pl.BlockSpec((seq, g * dh), lambda h, b: (b, ng + h)),
        pl.BlockSpec((seq, g * dh), lambda h, b: (b, 2 * ng + h)),
        pl.BlockSpec((n_tab, n_delta, 2 * t), (lambda h, b: (h, 0, 0)) if per_head else (lambda h, b: (0, 0, 0))),
    ]
    args = [qkv, qkv, qkv, strips]
    scratch = [
        pltpu.VMEM((n_tab, n_delta, t, t), F32),
        pltpu.VMEM((g, dh + DEN_ROWS, seq), BF16),
        pltpu.VMEM((g * SCORE_BUFFERS * n_maps, seq, t), F32),
    ]
    if kind == "moba":
        scratch.append(pltpu.VMEM((g, seq // t, dh), F32))
    elif kind == "diff":
        lam, subln = extra
        in_specs += [pl.BlockSpec(lam.shape, lambda h, b: (0, 0)),
                     pl.BlockSpec((1, dh), lambda h, b: (0, 0))]
        args += [lam, subln.reshape(1, dh)]
    elif kind == "fox":
        csum, csum_t = extra
        in_specs += [pl.BlockSpec((None, seq, dh), lambda h, b: (b, 0, 0)),
                     pl.BlockSpec((None, g, seq // t, t), lambda h, b: (b, h, 0, 0))]
        args += [csum, csum_t]
        scratch.append(pltpu.VMEM((g, seq, dh), F32))
    return pl.pallas_call(
        functools.partial(_attn_kernel, kind=kind, g=g, t=t, seq=seq, n_delta=n_delta,
                          last_uniform=last_uniform, lambda_init=lambda_init),
        grid=(ng, batch),
        in_specs=in_specs,
        out_specs=pl.BlockSpec((seq, g * dh), lambda h, b: (b, h)),
        out_shape=jax.ShapeDtypeStruct((m, N_HEADS * dh), BF16),
        scratch_shapes=scratch,
        compiler_params=_params(("parallel", "arbitrary")),
        name=kind + "_attention",
    )(*args)


def _fox_gate_kernel(f_ref, b_ref, cs_ref, cst_ref):
    z = f_ref[...] + b_ref[...]
    x = jnp.minimum(z, 0.0) - jnp.log1p(jnp.exp(-jnp.abs(z)))
    row = lax.broadcasted_iota(jnp.int32, x.shape, 0)
    shift = 1
    while shift < x.shape[0]:
        x = x + jnp.where(row >= shift, pltpu.roll(x, shift, 0), 0.0)
        shift *= 2
    x = x * LOG2E
    cs_ref[...] = x
    cst_ref[...] = x.T


def fox_gate(f, b_pad):
    batch, seq, lanes = f.shape
    return pl.pallas_call(
        _fox_gate_kernel,
        grid=(batch,),
        in_specs=[pl.BlockSpec((None, seq, lanes), lambda b: (b, 0, 0)),
                  pl.BlockSpec((1, lanes), lambda b: (0, 0))],
        out_specs=[pl.BlockSpec((None, seq, lanes), lambda b: (b, 0, 0)),
                   pl.BlockSpec((None, lanes, seq), lambda b: (b, 0, 0))],
        out_shape=[jax.ShapeDtypeStruct((batch, seq, lanes), F32),
                   jax.ShapeDtypeStruct((batch, lanes, seq), F32)],
        compiler_params=_params(("parallel",)),
        name="fox_gate_cumsum",
    )(f, b_pad)


def kernel(x, rel_bias, norm_gains, w_out, w_up, conv_w, conv_b, w_down, w_in_dil, w_in_moba,
           w_in_diff, lambda_diff, subln_diff, w_in_fox, b_fox):
    batch, seq, d = x.shape
    depth = norm_gains.shape[0]
    t = ATTN_TILE
    assert seq % t == 0 and t == MOBA_BLOCK and d == N_HEADS * HEAD_DIM
    causal3 = _strip_distances(3, t)
    causal_strips, causal_uniform = _bias_strips(rel_bias, causal3, _causal_log_mult(causal3))

    def qkv_scale(head_scale):
        return jnp.concatenate([jnp.full((d,), head_scale * LOG2E, F32), jnp.ones((2 * d,), F32)])

    w_out_b = cast_bf16(w_out, 1)

    h = x.reshape(batch * seq, d)
    u = rms_norm_bf16(h, norm_gains[0, 0])
    for i in range(depth):
        mixer, j = i % N_MIXERS, i // N_MIXERS
        g = norm_gains[i]
        if mixer == 0:
            qkv = proj_matmul(u, w_in_dil, j, qkv_scale(HEAD_DIM ** -0.5))
            dist = _strip_distances(4, t)
            strips, uniform = _bias_strips(rel_bias, dist, _dilated_log_mult(dist))
            y = attention("dilated", qkv, strips, uniform, batch, seq)
        elif mixer == 1:
            qkv = proj_matmul(u, w_in_moba, j, qkv_scale(HEAD_DIM ** -0.5))
            y = attention("moba", qkv, causal_strips, causal_uniform, batch, seq)
        elif mixer == 2:
            qkv = proj_matmul(u, w_in_diff, j, qkv_scale(DIFF_HEAD_DIM ** -0.5))
            y = attention("diff", qkv, causal_strips, causal_uniform, batch, seq,
                          extra=(lambda_diff[j], subln_diff[j]), lambda_init=0.8 - 0.6 * math.exp(-0.3 * i))
        else:
            w_fox_t = jnp.swapaxes(w_in_fox, 1, 2)
            qkv = proj_matmul(u, w_fox_t, j, qkv_scale(HEAD_DIM ** -0.5), w_transposed=True)
            w_f = jnp.pad(w_fox_t[j, 3 * d:, :], ((0, HEAD_DIM - N_HEADS), (0, 0)))[None]
            f = proj_matmul(u, w_f, 0, jnp.ones((HEAD_DIM,), F32), w_transposed=True, out_dtype=F32)
            b_pad = jnp.pad(b_fox[j], (0, HEAD_DIM - N_HEADS)).reshape(1, HEAD_DIM)
            csum, csum_t = fox_gate(f.reshape(batch, seq, HEAD_DIM), b_pad)
            dist = _strip_distances(2, t)
            strips, uniform = _bias_strips(None, dist, _causal_log_mult(dist))
            y = attention("fox", qkv, strips, uniform, batch, seq,
                          extra=(csum, csum_t.reshape(batch, HEAD_DIM, seq // t, t)))
        h, u = matmul_norm_res(y, w_out_b, 0, g[1], h, g[2])
        side = [(w_down, i)] + ([(w_out, i + 1)] if i + 1 < depth else [])
        hidden, w_down_b, *w_out_next = ffn_up(u, w_up, i, conv_w[i], conv_b[i], seq, side)
        if w_out_next:
            w_out_b = w_out_next[0]
        half = hidden.shape[1] // 2
        first_half = partial_matmul(hidden, w_down_b, 0, 0, half)
        h, u = matmul_norm_res(hidden, w_down_b, 0, g[3], h, norm_gains[i + 1, 0] if i + 1 < depth else None,
                               partial=first_half, k_block=1, tk=half, tm=256)
    return h.reshape(batch, seq, d)
```

```python
import functools
import math

import jax
import jax.numpy as jnp
import numpy as np
from jax import lax
from jax.experimental import pallas as pl
from jax.experimental.pallas import tpu as pltpu

N_HEADS = 16
HEAD_DIM = 128
N_MIXERS = 4
CONV_WIDTH = 3
RMS_EPS = 1e-6
N_BUCKETS = 32
MAX_DISTANCE = 128
DILATED_GROUPS = ((128, 1), (512, 4), (2048, 16))
MOBA_BLOCK = 256
MOBA_TOPK = 3
DIFF_HEAD_DIM = HEAD_DIM // 2
NEG = -1e30
LOG2E = math.log2(math.e)

ATTN_TILE = 256
ATTN_STREAMS_PER_STEP = 4
SCORE_LOOKAHEAD = 1
SCORE_BUFFERS = SCORE_LOOKAHEAD + 1
DEN_ROWS = 16
CONV_HALO = 16
FINISH_ROW_CHUNK = 128
CAST_BLOCK_ELEMS = 2 * 1024 * 1024
VMEM_LIMIT = 52 * 1024 * 1024

BF16 = jnp.bfloat16
F32 = jnp.float32


def _rms_rows(x, g):
    return x * lax.rsqrt(jnp.mean(x * x, axis=-1, keepdims=True) + RMS_EPS) * g


def _params(sem, vmem=VMEM_LIMIT):
    return pltpu.CompilerParams(dimension_semantics=sem, vmem_limit_bytes=vmem)


def _cast_kernel(x_ref, o_ref):
    o_ref[...] = x_ref[...].astype(o_ref.dtype)


def cast_bf16(w, layers):
    _, k, n = w.shape
    rows = 1 << (min(k, CAST_BLOCK_ELEMS // n).bit_length() - 1)
    assert k % rows == 0
    spec = pl.BlockSpec((None, rows, n), lambda l, r: (l, r, 0))
    return pl.pallas_call(
        _cast_kernel,
        grid=(layers, k // rows),
        in_specs=[spec],
        out_specs=spec,
        out_shape=jax.ShapeDtypeStruct((layers, k, n), BF16),
        compiler_params=_params(("parallel", "parallel")),
        name="cast_bf16",
    )(w)


def _rms_norm_kernel(x_ref, g_ref, o_ref):
    o_ref[...] = _rms_rows(x_ref[...], g_ref[...]).astype(o_ref.dtype)


def rms_norm_bf16(x, g, *, tm=512):
    m, d = x.shape
    return pl.pallas_call(
        _rms_norm_kernel,
        grid=(m // tm,),
        in_specs=[pl.BlockSpec((tm, d), lambda i: (i, 0)), pl.BlockSpec((1, d), lambda i: (0, 0))],
        out_specs=pl.BlockSpec((tm, d), lambda i: (i, 0)),
        out_shape=jax.ShapeDtypeStruct((m, d), BF16),
        compiler_params=_params(("parallel",)),
        name="rms_norm_bf16",
    )(x, g.reshape(1, d))


def _proj_kernel(u_ref, w_ref, cs_ref, o_ref, wb_ref, *, w_transposed):
    @pl.when(pl.program_id(1) == 0)
    def _():
        wb = w_ref[...].astype(BF16)
        wb_ref[...] = wb.T if w_transposed else wb

    acc = jnp.dot(u_ref[...], wb_ref[...], preferred_element_type=F32)
    o_ref[...] = (acc * cs_ref[...]).astype(o_ref.dtype)


def proj_matmul(u, w, layer, col_scale, *, w_transposed=False, out_dtype=BF16, tm=1024, tn=1024):
    m, d = u.shape
    n = col_scale.shape[0]
    tn = min(tn, n)
    w_spec = (pl.BlockSpec((None, tn, d), lambda j, i: (layer, j, 0)) if w_transposed
              else pl.BlockSpec((None, d, tn), lambda j, i: (layer, 0, j)))
    return pl.pallas_call(
        functools.partial(_proj_kernel, w_transposed=w_transposed),
        grid=(n // tn, m // tm),
        in_specs=[
            pl.BlockSpec((tm, d), lambda j, i: (i, 0)),
            w_spec,
            pl.BlockSpec((1, tn), lambda j, i: (0, j)),
        ],
        out_specs=pl.BlockSpec((tm, tn), lambda j, i: (i, j)),
        out_shape=jax.ShapeDtypeStruct((m, n), out_dtype),
        scratch_shapes=[pltpu.VMEM((d, tn), BF16)],
        compiler_params=_params(("parallel", "arbitrary")),
        name="proj_matmul",
    )(u, w, col_scale.reshape(1, n))


def _matmul_norm_res_kernel(y_ref, w_ref, g_ref, res_ref, *rest, has_partial, has_next):
    rest = list(rest)
    part_ref = rest.pop(0) if has_partial else None
    gn_ref = rest.pop(0) if has_next else None
    o_ref = rest.pop(0)
    u_ref = rest.pop(0) if has_next else None
    for r in range(0, o_ref.shape[0], FINISH_ROW_CHUNK):
        rows = slice(r, r + FINISH_ROW_CHUNK)
        acc = jnp.dot(y_ref[rows, :], w_ref[...], preferred_element_type=F32)
        if has_partial:
            acc = acc + part_ref[rows, :]
        h = res_ref[rows, :] + _rms_rows(acc, g_ref[...])
        o_ref[rows, :] = h
        if has_next:
            u_ref[rows, :] = _rms_rows(h, gn_ref[...]).astype(u_ref.dtype)


def _resident_weight_specs(w, layer, k_block, tm, tk):
    return [pl.BlockSpec((tm, tk), lambda i: (i, k_block)),
            pl.BlockSpec((None, tk, w.shape[2]), lambda i: (layer, k_block, 0), pipeline_mode=pl.Buffered(1))]


def matmul_norm_res(y, w, layer, g, res, g_next=None, *, partial=None, k_block=0, tk=None, tm=512):
    m = y.shape[0]
    d = w.shape[2]
    tk = y.shape[1] if tk is None else tk
    has_partial, has_next = partial is not None, g_next is not None
    row_spec = pl.BlockSpec((tm, d), lambda i: (i, 0))
    gain_spec = pl.BlockSpec((1, d), lambda i: (0, 0))
    in_specs = _resident_weight_specs(w, layer, k_block, tm, tk) + [gain_spec, row_spec]
    args = [y, w, g.reshape(1, d), res]
    out_shape = [jax.ShapeDtypeStruct((m, d), F32)]
    out_specs = [row_spec]
    if has_partial:
        in_specs.append(row_spec)
        args.append(partial)
    if has_next:
        in_specs.append(gain_spec)
        args.append(g_next.reshape(1, d))
        out_shape.append(jax.ShapeDtypeStruct((m, d), BF16))
        out_specs.append(row_spec)
    outs = pl.pallas_call(
        functools.partial(_matmul_norm_res_kernel, has_partial=has_partial, has_next=has_next),
        grid=(m // tm,),
        in_specs=in_specs,
        out_specs=out_specs,
        out_shape=out_shape,
        compiler_params=_params(("parallel",)),
        name="matmul_norm_res",
    )(*args)
    return outs if has_next else (outs[0], None)


def _partial_matmul_kernel(y_ref, w_ref, o_ref):
    o_ref[...] = jnp.dot(y_ref[...], w_ref[...], preferred_element_type=F32)


def partial_matmul(y, w, layer, k_block, tk, *, tm=512):
    m = y.shape[0]
    d = w.shape[2]
    return pl.pallas_call(
        _partial_matmul_kernel,
        grid=(m // tm,),
        in_specs=_resident_weight_specs(w, layer, k_block, tm, tk),
        out_specs=pl.BlockSpec((tm, d), lambda i: (i, 0)),
        out_shape=jax.ShapeDtypeStruct((m, d), F32),
        compiler_params=_params(("parallel",)),
        name="partial_matmul",
    )(y, w)


def _gelu_tanh(x):
    return x * (0.5 * (1.0 + jnp.tanh(math.sqrt(2.0 / math.pi) * (x + 0.044715 * (x * x * x)))))


def _causal_conv3(a, tail, cw, cb):
    a_ext = jnp.concatenate([tail, a], axis=0)
    r1 = pltpu.roll(a_ext, 1, 0)
    r2 = pltpu.roll(a_ext, 2, 0)
    y = cw[0:1, :] * r2 + cw[1:2, :] * r1 + cw[2:3, :] * a_ext
    return y[CONV_HALO:, :] + cb


def _ffn_up_kernel(u_ref, wg_ref, wu_ref, cwg_ref, cwu_ref, cbg_ref, cbu_ref, *rest, tm, seq, n_side):
    side_in, rest = rest[:n_side], rest[n_side:]
    o_ref, side_out = rest[0], rest[1:1 + n_side]
    wgb_ref, wub_ref, carry_ref = rest[1 + n_side:]
    i = pl.program_id(1)

    for src_ref, dst_ref in zip(side_in, side_out):
        dst_ref[...] = src_ref[...].astype(dst_ref.dtype)

    @pl.when(i == 0)
    def _():
        wgb_ref[...] = wg_ref[...].astype(BF16)
        wub_ref[...] = wu_ref[...].astype(BF16)

    @pl.when((i * tm) % seq == 0)
    def _():
        carry_ref[...] = jnp.zeros(carry_ref.shape, F32)

    u = u_ref[...]
    act = []
    for idx, (wb_ref, cw_ref, cb_ref) in enumerate(((wgb_ref, cwg_ref, cbg_ref), (wub_ref, cwu_ref, cbu_ref))):
        a = jnp.dot(u, wb_ref[...], preferred_element_type=F32)
        act.append(_causal_conv3(a, carry_ref[idx], cw_ref[...], cb_ref[...]))
        carry_ref[idx] = a[tm - CONV_HALO:, :]
    o_ref[...] = (_gelu_tanh(act[0]) * act[1]).astype(o_ref.dtype)


def ffn_up(u, w_up, layer, conv_w, conv_b, seq, side_casts=(), *, tm=1024, tn=512):
    m, d = u.shape
    f = w_up.shape[2] // 2
    nn = f // tn
    nm = m // tm
    assert seq % tm == 0
    cb = conv_b.reshape(1, 2 * f)
    side_specs, side_shapes = [], []
    for w, l in side_casts:
        rows = w.shape[1] // (nn * nm)
        assert rows * nn * nm == w.shape[1] and rows % 16 == 0
        side_specs.append(pl.BlockSpec((None, rows, w.shape[2]), lambda j, i, l=l: (l, j * nm + i, 0)))
        side_shapes.append(jax.ShapeDtypeStruct((1,) + w.shape[1:], BF16))
    side_out_specs = [pl.BlockSpec(s.block_shape, lambda j, i: (0, j * nm + i, 0)) for s in side_specs]
    outs = pl.pallas_call(
        functools.partial(_ffn_up_kernel, tm=tm, seq=seq, n_side=len(side_casts)),
        grid=(nn, nm),
        in_specs=[
            pl.BlockSpec((tm, d), lambda j, i: (i, 0)),
            pl.BlockSpec((None, d, tn), lambda j, i: (layer, 0, j)),
            pl.BlockSpec((None, d, tn), lambda j, i: (layer, 0, nn + j)),
            pl.BlockSpec((CONV_WIDTH, tn), lambda j, i: (0, j)),
            pl.BlockSpec((CONV_WIDTH, tn), lambda j, i: (0, nn + j)),
            pl.BlockSpec((1, tn), lambda j, i: (0, j)),
            pl.BlockSpec((1, tn), lambda j, i: (0, nn + j)),
        ] + side_specs,
        out_specs=[pl.BlockSpec((tm, tn), lambda j, i: (i, j))] + side_out_specs,
        out_shape=[jax.ShapeDtypeStruct((m, f), BF16)] + side_shapes,
        scratch_shapes=[pltpu.VMEM((d, tn), BF16), pltpu.VMEM((d, tn), BF16),
                        pltpu.VMEM((2, CONV_HALO, tn), F32)],
        compiler_params=_params(("arbitrary", "arbitrary")),
        name="ffn_up_conv_glu",
    )(u, w_up, w_up, conv_w, conv_w, cb, cb, *[w for w, _ in side_casts])
    return outs


def _rel_bucket_np(dist):
    dist = np.maximum(dist, 0)
    max_exact = N_BUCKETS // 2
    d = np.maximum(dist, 1).astype(np.float32)
    large = max_exact + (np.log(d / max_exact) / math.log(MAX_DISTANCE / max_exact)
                         * (N_BUCKETS - max_exact)).astype(np.int32)
    large = np.minimum(large, N_BUCKETS - 1)
    return np.where(dist < max_exact, dist, large)


def _strip_distances(n_delta, t):
    return np.arange(n_delta)[:, None] * t + np.arange(2 * t)[None, :] - t


def _causal_log_mult(dist):
    return np.where(dist >= 0, 0.0, -np.inf)


def _dilated_log_mult(dist):
    mult = np.zeros(dist.shape, np.float64)
    for window, dil in DILATED_GROUPS:
        mult += (dist >= 0) & (dist <= window) & (dist % dil == 0)
    with np.errstate(divide="ignore"):
        return np.log(mult)


def _bias_strips(rel_bias, dist, log_mult):
    shown = np.isfinite(log_mult)
    bucket = _rel_bucket_np(dist)
    last_uniform = bool(shown[-1].all() and (log_mult[-1] == log_mult[-1, 0]).all()
                        and (bucket[-1] == bucket[-1, 0]).all())
    offset = jnp.asarray(np.where(shown, log_mult * LOG2E, NEG), F32)
    if rel_bias is None:
        return offset[None], last_uniform
    bias = jnp.take(rel_bias.astype(F32), jnp.asarray(bucket), axis=0)
    strips = jnp.where(jnp.asarray(shown)[..., None], bias * LOG2E + offset[..., None], NEG)
    return jnp.moveaxis(strips, -1, 0), last_uniform


def _attn_kernel(*refs, kind, g, t, seq, n_delta, last_uniform, lambda_init):
    n_in = {"dilated": 4, "moba": 4, "diff": 6, "fox": 6}[kind]
    q_ref, k_ref, v_ref, strip_ref = refs[:4]
    o_ref = refs[n_in]
    tab_ref, vt_ref, s_ref = refs[n_in + 1:n_in + 4]
    hg = pl.program_id(0)
    nk = seq // t
    n_tab = tab_ref.shape[0]
    n_maps = s_ref.shape[0] // (SCORE_BUFFERS * g)
    dh = HEAD_DIM

    @pl.when(pl.program_id(1) == 0)
    def _():
        for hh in range(n_tab):
            for dl in range(n_delta):
                strip = jnp.broadcast_to(strip_ref[hh, dl:dl + 1, :], (t, 2 * t))
                tab_ref[hh, dl] = pltpu.roll(strip, 0, 1, stride=1, stride_axis=0)[:, t:]

    def tile(j):
        return slice(j * t, (j + 1) * t)

    def head_passes(hh):
        head = slice(hh * dh, (hh + 1) * dh)
        ones_row = lax.broadcasted_iota(jnp.int32, (DEN_ROWS, seq), 0) == 0
        vt_ref[hh, dh:, :] = jnp.where(ones_row, 1.0, 0.0).astype(BF16)
        for jb in range(nk):
            vt_ref[hh, :dh, tile(jb)] = v_ref[tile(jb), head].T
        if kind == "moba":
            kmean_ref = refs[n_in + 4]
            for jb in range(nk):
                kmean_ref[hh, jb:jb + 1, :] = jnp.mean(k_ref[tile(jb), head].astype(F32), axis=0, keepdims=True)
            km = kmean_ref[hh]
            km_hi = km.astype(BF16)
            km_lo = (km - km_hi.astype(F32)).astype(BF16)
        if kind == "fox":
            cs_ref, cst_ref, ckb_ref = refs[4], refs[5], refs[n_in + 4]
            cs = cs_ref[...]
            lane = lax.broadcasted_iota(jnp.int32, cs.shape, 1)
            ck = jnp.sum(jnp.where(lane == hg * g + hh, cs, 0.0), axis=-1, keepdims=True)
            ckb_ref[hh] = jnp.broadcast_to(ck, cs.shape)
        if kind == "diff":
            lam_ref, sub_ref = refs[4], refs[5]
            lam = lam_ref[...]
            lmbda = (jnp.exp(jnp.sum(lam[0:1, :] * lam[1:2, :], axis=-1, keepdims=True))
                     - jnp.exp(jnp.sum(lam[2:3, :] * lam[3:4, :], axis=-1, keepdims=True)) + lambda_init)
        far_term = strip_ref[hh % n_tab, n_delta - 1:n_delta, 0:1] if last_uniform else None

        def is_far(i, j):
            return last_uniform and i - j >= n_delta - 1

        def score_pass(i):
            qt = q_ref[tile(i), head].T
            if kind == "diff":
                row = lax.broadcasted_iota(jnp.int32, qt.shape, 0)
                qts = [jnp.where(row < DIFF_HEAD_DIM, qt, jnp.zeros_like(qt)),
                       jnp.where(row >= DIFF_HEAD_DIM, qt, jnp.zeros_like(qt))]
            else:
                qts = [qt]
            sel = None
            if kind == "moba" and i > 0:
                gate = (jnp.dot(km_hi, qt, preferred_element_type=F32)
                        + jnp.dot(km_lo, qt, preferred_element_type=F32))
                blk = lax.broadcasted_iota(jnp.int32, gate.shape, 0)
                sel = []
                for jb in range(i):
                    gj = gate[jb:jb + 1, :]
                    beats = (blk < i) & ((gate > gj) | ((gate == gj) & (blk < jb)))
                    rank = jnp.sum(jnp.where(beats, 1.0, 0.0), axis=0, keepdims=True)
                    sel.append(jnp.where(rank < MOBA_TOPK, 0.0, NEG))
            maxima = []
            for mp in range(n_maps):
                buf = (hh * SCORE_BUFFERS + i % SCORE_BUFFERS) * n_maps + mp
                m = None
                for j in range(i + 1):
                    s = jnp.dot(k_ref[tile(j), head], qts[mp], preferred_element_type=F32)
                    if kind == "fox":
                        s = s + cst_ref[hh, i:i + 1, :] - jnp.concatenate([ckb_ref[hh, tile(j), :]] * (t // dh), axis=1)
                    if not is_far(i, j):
                        s = s + tab_ref[hh % n_tab, min(i - j, n_delta - 1)]
                    if kind == "moba" and j < i:
                        s = s + sel[j]
                    s_ref[buf, tile(j), :] = s
                    tile_max = jnp.max(s, axis=0, keepdims=True)
                    if is_far(i, j):
                        tile_max = tile_max + far_term
                    m = tile_max if m is None else jnp.maximum(m, tile_max)
                maxima.append(m)
            return maxima

        def value_pass(i, maxima):
            outs = []
            for mp in range(n_maps):
                buf = (hh * SCORE_BUFFERS + i % SCORE_BUFFERS) * n_maps + mp
                acc = None
                for j in range(i + 1):
                    shift = maxima[mp] - far_term if is_far(i, j) else maxima[mp]
                    p = jnp.exp2(s_ref[buf, tile(j), :] - shift).astype(BF16)
                    part = jnp.dot(vt_ref[hh, :, tile(j)], p, preferred_element_type=F32)
                    acc = part if acc is None else acc + part
                outs.append(acc[:dh, :] / acc[dh:dh + 1, :])
            if kind == "diff":
                out = _rms_rows((outs[0] - lmbda * outs[1]).T, sub_ref[...]) * (1.0 - lambda_init)
            else:
                out = outs[0].T
            o_ref[tile(i), head] = out.astype(o_ref.dtype)

        return score_pass, value_pass

    passes = [head_passes(hh) for hh in range(g)]
    maxima = {}
    for i in range(-SCORE_LOOKAHEAD, nk):
        ahead = i + SCORE_LOOKAHEAD
        if ahead < nk:
            for hh, (score_pass, _) in enumerate(passes):
                maxima[hh, ahead] = score_pass(ahead)
        if i >= 0:
            for hh, (_, value_pass) in enumerate(passes):
                value_pass(i, maxima.pop((hh, i)))


def attention(kind, qkv, strips, last_uniform, batch, seq, extra=(), lambda_init=0.0):
    t, dh = ATTN_TILE, HEAD_DIM
    g = ATTN_STREAMS_PER_STEP // (2 if kind == "diff" else 1)
    ng = N_HEADS // g
    m = qkv.shape[0]
    per_head = strips.shape[0] == N_HEADS
    n_tab = g if per_head else 1
    n_delta = strips.shape[1]
    n_maps = 2 if kind == "diff" else 1
    in_specs = [
        pl.BlockSpec((seq, g * dh), lambda h, b: (b, h)),
        pl.BlockSpec((seq, g * dh), lambda h, b: (b, ng + h)),
        pl.BlockSpec((seq, g * dh), lambda h, b: (b, 2 * ng + h)),
        pl.BlockSpec((n_tab, n_delta, 2 * t), (lambda h, b: (h, 0, 0)) if per_head else (lambda h, b: (0, 0, 0))),
    ]
    args = [qkv, qkv, qkv, strips]
    scratch = [
        pltpu.VMEM((n_tab, n_delta, t, t), F32),
        pltpu.VMEM((g, dh + DEN_ROWS, seq), BF16),
        pltpu.VMEM((g * SCORE_BUFFERS * n_maps, seq, t), F32),
    ]
    if kind == "moba":
        scratch.append(pltpu.VMEM((g, seq // t, dh), F32))
    elif kind == "diff":
        lam, subln = extra
        in_specs += [pl.BlockSpec(lam.shape, lambda h, b: (0, 0)),
                     pl.BlockSpec((1, dh), lambda h, b: (0, 0))]
        args += [lam, subln.reshape(1, dh)]
    elif kind == "fox":
        csum, csum_t = extra
        in_specs += [pl.BlockSpec((None, seq, dh), lambda h, b: (b, 0, 0)),
                     pl.BlockSpec((None, g, seq // t, t), lambda h, b: (b, h, 0, 0))]
        args += [csum, csum_t]
        scratch.append(pltpu.VMEM((g, seq, dh), F32))
    return pl.pallas_call(
        functools.partial(_attn_kernel, kind=kind, g=g, t=t, seq=seq, n_delta=n_delta,
                          last_uniform=last_uniform, lambda_init=lambda_init),
        grid=(ng, batch),
        in_specs=in_specs,
        out_specs=pl.BlockSpec((seq, g * dh), lambda h, b: (b, h)),
        out_shape=jax.ShapeDtypeStruct((m, N_HEADS * dh), BF16),
        scratch_shapes=scratch,
        compiler_params=_params(("parallel", "arbitrary")),
        name=kind + "_attention",
    )(*args)


def _fox_gate_kernel(f_ref, b_ref, cs_ref, cst_ref):
    z = f_ref[...] + b_ref[...]
    x = jnp.minimum(z, 0.0) - jnp.log1p(jnp.exp(-jnp.abs(z)))
    row = lax.broadcasted_iota(jnp.int32, x.shape, 0)
    shift = 1
    while shift < x.shape[0]:
        x = x + jnp.where(row >= shift, pltpu.roll(x, shift, 0), 0.0)
        shift *= 2
    x = x * LOG2E
    cs_ref[...] = x
    cst_ref[...] = x.T


def fox_gate(f, b_pad):
    batch, seq, lanes = f.shape
    return pl.pallas_call(
        _fox_gate_kernel,
        grid=(batch,),
        in_specs=[pl.BlockSpec((None, seq, lanes), lambda b: (b, 0, 0)),
                  pl.BlockSpec((1, lanes), lambda b: (0, 0))],
        out_specs=[pl.BlockSpec((None, seq, lanes), lambda b: (b, 0, 0)),
                   pl.BlockSpec((None, lanes, seq), lambda b: (b, 0, 0))],
        out_shape=[jax.ShapeDtypeStruct((batch, seq, lanes), F32),
                   jax.ShapeDtypeStruct((batch, lanes, seq), F32)],
        compiler_params=_params(("parallel",)),
        name="fox_gate_cumsum",
    )(f, b_pad)


def kernel(x, rel_bias, norm_gains, w_out, w_up, conv_w, conv_b, w_down, w_in_dil, w_in_moba,
           w_in_diff, lambda_diff, subln_diff, w_in_fox, b_fox):
    batch, seq, d = x.shape
    depth = norm_gains.shape[0]
    t = ATTN_TILE
    assert seq % t == 0 and t == MOBA_BLOCK and d == N_HEADS * HEAD_DIM
    causal3 = _strip_distances(3, t)
    causal_strips, causal_uniform = _bias_strips(rel_bias, causal3, _causal_log_mult(causal3))

    def qkv_scale(head_scale):
        return jnp.concatenate([jnp.full((d,), head_scale * LOG2E, F32), jnp.ones((2 * d,), F32)])

    w_out_b = cast_bf16(w_out, 1)

    h = x.reshape(batch * seq, d)
    u = rms_norm_bf16(h, norm_gains[0, 0])
    for i in range(depth):
        mixer, j = i % N_MIXERS, i // N_MIXERS
        g = norm_gains[i]
        if mixer == 0:
            qkv = proj_matmul(u, w_in_dil, j, qkv_scale(HEAD_DIM ** -0.5))
            dist = _strip_distances(4, t)
            strips, uniform = _bias_strips(rel_bias, dist, _dilated_log_mult(dist))
            y = attention("dilated", qkv, strips, uniform, batch, seq)
        elif mixer == 1:
            qkv = proj_matmul(u, w_in_moba, j, qkv_scale(HEAD_DIM ** -0.5))
            y = attention("moba", qkv, causal_strips, causal_uniform, batch, seq)
        elif mixer == 2:
            qkv = proj_matmul(u, w_in_diff, j, qkv_scale(DIFF_HEAD_DIM ** -0.5))
            y = attention("diff", qkv, causal_strips, causal_uniform, batch, seq,
                          extra=(lambda_diff[j], subln_diff[j]), lambda_init=0.8 - 0.6 * math.exp(-0.3 * i))
        else:
            w_fox_t = jnp.swapaxes(w_in_fox, 1, 2)
            qkv = proj_matmul(u, w_fox_t, j, qkv_scale(HEAD_DIM ** -0.5), w_transposed=True)
            w_f = jnp.pad(w_fox_t[j, 3 * d:, :], ((0, HEAD_DIM - N_HEADS), (0, 0)))[None]
            f = proj_matmul(u, w_f, 0, jnp.ones((HEAD_DIM,), F32), w_transposed=True, out_dtype=F32)
            b_pad = jnp.pad(b_fox[j], (0, HEAD_DIM - N_HEADS)).reshape(1, HEAD_DIM)
            csum, csum_t = fox_gate(f.reshape(batch, seq, HEAD_DIM), b_pad)
            dist = _strip_distances(2, t)
            strips, uniform = _bias_strips(None, dist, _causal_log_mult(dist))
            y = attention("fox", qkv, strips, uniform, batch, seq,
                          extra=(csum, csum_t.reshape(batch, HEAD_DIM, seq // t, t)))
        h, u = matmul_norm_res(y, w_out_b, 0, g[1], h, g[2])
        side = [(w_down, i)] + ([(w_out, i + 1)] if i + 1 < depth else [])
        hidden, w_down_b, *w_out_next = ffn_up(u, w_up, i, conv_w[i], conv_b[i], seq, side)
        if w_out_next:
            w_out_b = w_out_next[0]
        tail = hidden.shape[1] // 4
        lead = partial_matmul(hidden, w_down_b, 0, 0, 3 * tail)
        h, u = matmul_norm_res(hidden, w_down_b, 0, g[3], h, norm_gains[i + 1, 0] if i + 1 < depth else None,
                               partial=lead, k_block=3, tk=tail)
    return h.reshape(batch, seq, d)
```
